```python
import jax, jax.numpy as jnp
from jax import lax
import numpy as np

D_MODEL = 1024
BATCH = 8
SEQ = 4096
DEPTH = 4

GRID_W = 64
CTX_LEN = 256
ROPE_BASE = 10000.0
Q_BLOCK = 128
EPS = 1e-6

MLA_HEADS = D_MODEL // 128
MLA_NOPE = 64
MLA_ROPE = 32
MLA_QK = MLA_NOPE + MLA_ROPE
MLA_V = 64
Q_LORA = 3 * D_MODEL // 8
KV_LORA = D_MODEL // 4
MLA_IN = Q_LORA + KV_LORA + MLA_ROPE
CONV_CH = D_MODEL // 2
CONV_K = 31
EVEN_IN = MLA_IN + 2 * CONV_CH
EVEN_OUT = MLA_HEADS * MLA_V + CONV_CH

RW_N = 64
RW_HEADS = D_MODEL // 128
RW_DIM = RW_HEADS * RW_N
DECAY_LORA = 64
ICLR_LORA = 64
GATE_LORA = 128
SHIFT_K = 3
RW_IN = 3 * RW_DIM + 2 * DECAY_LORA + 2 * ICLR_LORA + GATE_LORA
GN_EPS = 64e-5

NA_HEADS = D_MODEL // 128
NA_DIM = 64
WIN_H = 8
WIN_W = 16
NA_IN = 3 * NA_HEADS * NA_DIM
ODD_IN = RW_IN + NA_IN
ODD_OUT = RW_DIM + NA_HEADS * NA_DIM

N_EXPERTS = 16
D_EXPERT = D_MODEL
EC_FACTOR = 2

kernel_name = 'hybrid_mla_conformer_rwkv7_natten_ecmoe_dit'


def rms_norm(x, g):
    xf = x.astype(jnp.float32)
    y = xf * lax.rsqrt(jnp.mean(jnp.square(xf), -1, keepdims=True) + EPS)
    return (y * g.astype(jnp.float32)).astype(x.dtype)


def layer_norm(x, g, b):
    xf = x.astype(jnp.float32)
    mu = jnp.mean(xf, -1, keepdims=True)
    var = jnp.mean(jnp.square(xf - mu), -1, keepdims=True)
    y = (xf - mu) * lax.rsqrt(var + EPS)
    return (y * g.astype(jnp.float32) + b.astype(jnp.float32)).astype(x.dtype)


def head_group_norm(y, g, b):
    B, T, H, N = y.shape
    mu = jnp.mean(y, -1, keepdims=True)
    var = jnp.mean(jnp.square(y - mu), -1, keepdims=True)
    yn = ((y - mu) * lax.rsqrt(var + GN_EPS)).reshape(B, T, H * N)
    return yn * g + b


def depthwise_conv(x, w):
    pad = (w.shape[0] - 1) // 2
    return lax.conv_general_dilated(x, w[:, None, :].astype(x.dtype), (1,), [(pad, pad)],
                                    dimension_numbers=('NWC', 'WIO', 'NWC'),
                                    feature_group_count=x.shape[-1])


def axial_rope_tables(n_tok, rot_dim):
    t = jnp.arange(n_tok, dtype=jnp.int32)
    row = (t // GRID_W).astype(jnp.float32)
    col = (t % GRID_W).astype(jnp.float32)
    half = rot_dim // 2
    inv = ROPE_BASE ** (-jnp.arange(0, half, 2, dtype=jnp.float32) / half)
    ar = row[:, None] * inv
    ac = col[:, None] * inv
    ang = jnp.concatenate([ar, ar, ac, ac], -1)
    return jnp.cos(ang), jnp.sin(ang)


def rotate_half(u):
    u1, u2 = jnp.split(u, 2, -1)
    return jnp.concatenate([-u2, u1], -1)


def apply_axial_rope(x, cos, sin):
    half = x.shape[-1] // 2
    rot = jnp.concatenate([rotate_half(x[..., :half]), rotate_half(x[..., half:])], -1)
    return (x * cos[:, None, :] + rot * sin[:, None, :]).astype(x.dtype)


def dense_attention(q, k, v):
    scale = q.shape[-1] ** -0.5
    s = jnp.einsum('bqhd,bkhd->bhqk', q, k).astype(jnp.float32) * scale
    p = jax.nn.softmax(s, -1).astype(v.dtype)
    return jnp.einsum('bhqk,bkhd->bqhd', p, v)


def blocked_attention(q, k, v):
    B, T, H, D = q.shape
    nb = T // Q_BLOCK
    qb = jnp.moveaxis(q.reshape(B, nb, Q_BLOCK, H, D), 1, 0)
    out = lax.map(lambda qq: dense_attention(qq, k, v), qb)
    return jnp.moveaxis(out, 0, 1).reshape(B, T, H, v.shape[-1])


def mla_qkv(u, q_norm, w_uq, kv_norm, w_ukv, q_g, k_g, rope_cs=None):
    B, T, _ = u.shape
    cq = u[..., :Q_LORA]
    ckv = u[..., Q_LORA:Q_LORA + KV_LORA]
    kr = u[..., Q_LORA + KV_LORA:MLA_IN]
    q = (rms_norm(cq, q_norm) @ w_uq).reshape(B, T, MLA_HEADS, MLA_QK)
    kv = (rms_norm(ckv, kv_norm) @ w_ukv).reshape(B, T, MLA_HEADS, MLA_NOPE + MLA_V)
    k_nope, v = kv[..., :MLA_NOPE], kv[..., MLA_NOPE:]
    k = jnp.concatenate([k_nope, jnp.broadcast_to(kr[:, :, None, :], (B, T, MLA_HEADS, MLA_ROPE))], -1)
    q = rms_norm(q, q_g)
    k = rms_norm(k, k_g)
    if rope_cs is not None:
        cos, sin = rope_cs
        q = jnp.concatenate([q[..., :MLA_NOPE], apply_axial_rope(q[..., MLA_NOPE:], cos, sin)], -1)
        k = jnp.concatenate([k[..., :MLA_NOPE], apply_axial_rope(k[..., MLA_NOPE:], cos, sin)], -1)
    return q, k, v


def conformer_conv(u, dw_w, dw_b, ln_g, ln_b):
    a, gate = jnp.split(u, 2, -1)
    h = a * jax.nn.sigmoid(gate)
    h = depthwise_conv(h, dw_w) + dw_b
    return jax.nn.silu(layer_norm(h, ln_g, ln_b))


def even_mixer(h, hc, rope_cs, w_in, w_out, q_norm, w_uq, kv_norm, w_ukv, q_g, k_g,
               dw_w, dw_b, ln_g, ln_b, with_ctx_out):
    B, T, _ = h.shape
    u = h @ w_in
    uc = hc @ w_in
    mla_p = (q_norm, w_uq, kv_norm, w_ukv, q_g, k_g)
    q, k, v = mla_qkv(u[..., :MLA_IN], *mla_p, rope_cs=rope_cs)
    qc, kc, vc = mla_qkv(uc[..., :MLA_IN], *mla_p)
    o_att = blocked_attention(q, jnp.concatenate([k, kc], 1), jnp.concatenate([v, vc], 1))
    o_conv = conformer_conv(u[..., MLA_IN:], dw_w, dw_b, ln_g, ln_b)
    y = jnp.concatenate([o_att.reshape(B, T, -1), o_conv], -1) @ w_out
    if not with_ctx_out:
        return y, None
    oc_att = dense_attention(qc, kc, vc)
    oc_conv = conformer_conv(uc[..., MLA_IN:], dw_w, dw_b, ln_g, ln_b)
    yc = jnp.concatenate([oc_att.reshape(B, hc.shape[1], -1), oc_conv], -1) @ w_out
    return y, yc


def wkv7_step(S, inp):
    r, w, k, v, a, b = inp
    sa = jnp.einsum('bhvk,bhk->bhv', S, a)
    S = S * w[:, :, None, :] + sa[..., None] * b[:, :, None, :] + v[..., None] * k[:, :, None, :]
    return S, jnp.einsum('bhvk,bhk->bhv', S, r)


def wkv7_scan(S0, r, w, k, v, a, b, reverse):
    xs = tuple(jnp.moveaxis(t.astype(jnp.float32), 1, 0) for t in (r, w, k, v, a, b))
    S, ys = lax.scan(wkv7_step, S0, xs, reverse=reverse)
    return S, jnp.moveaxis(ys, 0, 1)


def rwkv_inputs(u, shift_w, w0, w2, a0, a2, g2, k_k, k_a):
    u = depthwise_conv(u, shift_w)
    B, T, _ = u.shape
    o1 = 3 * RW_DIM
    o2 = o1 + 2 * DECAY_LORA
    o3 = o2 + 2 * ICLR_LORA
    r, k, v = jnp.split(u[..., :o1], 3, axis=-1)
    lw = u[..., o1:o2].reshape(B, T, 2, DECAY_LORA)
    la = u[..., o2:o3].reshape(B, T, 2, ICLR_LORA)
    lg = u[..., o3:]
    logw = (w0 + jnp.einsum('btdr,drc->btdc', jnp.tanh(lw), w2)).astype(jnp.float32)
    w = jnp.exp(-jnp.exp(-jax.nn.softplus(-logw) - 0.5))
    a = jax.nn.sigmoid((a0 + jnp.einsum('btdr,drc->btdc', la, a2)).astype(jnp.float32))
    kk = (k * k_k).reshape(B, T, RW_HEADS, RW_N).astype(jnp.float32)
    kk = kk * lax.rsqrt(jnp.sum(kk * kk, -1, keepdims=True) + 1e-12)
    k_eff = k[:, :, None, :].astype(jnp.float32) * (1.0 + (a - 1.0) * k_a)
    g = jax.nn.sigmoid(lg) @ g2
    hd = lambda t: t.reshape(B, T, RW_HEADS, RW_N).astype(jnp.float32)
    dir_hd = lambda t: t.reshape(B, T, 2, RW_HEADS, RW_N)
    return hd(r), hd(v), kk, dir_hd(w), dir_hd(a), dir_hd(k_eff), g


def rwkv_mixer(u, uc, shift_w, w0, w2, a0, a2, g2, k_k, k_a, r_k, gn_g, gn_b, with_ctx_out):
    p = (shift_w, w0, w2, a0, a2, g2, k_k, k_a)
    lat = rwkv_inputs(u, *p)
    cx = rwkv_inputs(uc, *p)
    rk = r_k.reshape(RW_HEADS, RW_N).astype(jnp.float32)
    B = u.shape[0]

    def direction(inp, d, S0):
        r, v, kk, w, a, k_eff, _ = inp
        a_d, k_d = a[:, :, d], k_eff[:, :, d]
        S, y = wkv7_scan(S0, r, w[:, :, d], k_d, v, -kk, kk * a_d, reverse=(d == 1))
        return S, y + jnp.sum(r * k_d * rk, -1, keepdims=True) * v

    y_lat, y_ctx = [], []
    for d in range(2):
        S0 = jnp.zeros((B, RW_HEADS, RW_N, RW_N), jnp.float32)
        S_c, yc_d = direction(cx, d, S0)
        _, yl_d = direction(lat, d, S_c)
        y_lat.append(yl_d)
        y_ctx.append(yc_d)
    y = head_group_norm(y_lat[0] + y_lat[1], gn_g, gn_b) * lat[6]
    if not with_ctx_out:
        return y, None
    yc = head_group_norm(y_ctx[0] + y_ctx[1], gn_g, gn_b) * cx[6]
    return y, yc


def natten_qkv(u, q_g, k_g):
    B, T, _ = u.shape
    q, k, v = [t.reshape(B, T, NA_HEADS, NA_DIM) for t in jnp.split(u, 3, -1)]
    return rms_norm(q, q_g), rms_norm(k, k_g), v


def neighbourhood_attention(q, k, v, k_ctx, v_ctx, rpb):
    B, T, H, Dh = q.shape
    rows = T // GRID_W
    kh = min(WIN_H, rows)
    kw = min(WIN_W, GRID_W)
    n_loc = kh * kw
    scale = Dh ** -0.5
    qg = q.reshape(B, rows, GRID_W, H, Dh)
    kg = k.reshape(B, rows, GRID_W, H, Dh)
    vg = v.reshape(B, rows, GRID_W, H, Dh)
    cols = jnp.arange(GRID_W, dtype=jnp.int32)
    col_start = jnp.clip(cols - kw // 2, 0, GRID_W - kw)
    col_idx = col_start[:, None] + jnp.arange(kw, dtype=jnp.int32)[None, :]
    dcol = col_idx - cols[:, None] + (WIN_W - 1)

    def one_row(r):
        rs = jnp.clip(r - kh // 2, 0, rows - kh)
        k_band = lax.dynamic_slice_in_dim(kg, rs, kh, axis=1)
        v_band = lax.dynamic_slice_in_dim(vg, rs, kh, axis=1)
        k_win = k_band[:, :, col_idx]
        v_win = v_band[:, :, col_idx]
        q_row = lax.dynamic_index_in_dim(qg, r, axis=1, keepdims=False)
        drow = rs + jnp.arange(kh, dtype=jnp.int32) - r + (WIN_H - 1)
        bias = jnp.transpose(rpb[:, drow][:, :, dcol], (0, 2, 1, 3))
        s_loc = jnp.einsum('bchd,bicjhd->bhcij', q_row, k_win).astype(jnp.float32) * scale + bias.astype(jnp.float32)
        s_ctx = jnp.einsum('bchd,bnhd->bhcn', q_row, k_ctx).astype(jnp.float32) * scale
        s = jnp.concatenate([s_loc.reshape(B, H, GRID_W, n_loc), s_ctx], -1)
        p = jax.nn.softmax(s, -1).astype(v.dtype)
        o = jnp.einsum('bhcij,bicjhd->bchd', p[..., :n_loc].reshape(B, H, GRID_W, kh, kw), v_win)
        return o + jnp.einsum('bhcn,bnhd->bchd', p[..., n_loc:], v_ctx)

    out = lax.map(one_row, jnp.arange(rows, dtype=jnp.int32))
    return jnp.moveaxis(out, 0, 1).reshape(B, T, H, Dh)


def odd_mixer(h, hc, w_in, w_out, shift_w, w0, w2, a0, a2, g2, k_k, k_a, r_k, gn_g, gn_b,
              q_g, k_g, rpb, with_ctx_out):
    B, T, _ = h.shape
    u = h @ w_in
    uc = hc @ w_in
    y_rw, yc_rw = rwkv_mixer(u[..., :RW_IN], uc[..., :RW_IN], shift_w, w0, w2, a0, a2, g2,
                             k_k, k_a, r_k, gn_g, gn_b, with_ctx_out)
    q, k, v = natten_qkv(u[..., RW_IN:], q_g, k_g)
    qc, kc, vc = natten_qkv(uc[..., RW_IN:], q_g, k_g)
    y_na = neighbourhood_attention(q, k, v, kc, vc, rpb)
    y = jnp.concatenate([y_rw, y_na.reshape(B, T, -1)], -1) @ w_out
    if not with_ctx_out:
        return y, None
    yc_na = dense_attention(qc, kc, vc)
    yc = jnp.concatenate([yc_rw, yc_na.reshape(B, hc.shape[1], -1)], -1) @ w_out
    return y, yc


def expert_choice_ffn(h, router, w1, w3, w2):
    B, T, D = h.shape
    cap = max(1, EC_FACTOR * T // N_EXPERTS)
    aff = jax.nn.softmax(jnp.einsum('btd,de->bte', h, router).astype(jnp.float32), -1)
    gate, idx = lax.top_k(jnp.swapaxes(aff, 1, 2), cap)
    xs = jax.vmap(lambda hb, ib: hb[ib])(h, idx)
    hid = jax.nn.silu(jnp.einsum('becd,edf->becf', xs, w1)) * jnp.einsum('becd,edf->becf', xs, w3)
    ys = jnp.einsum('becf,efd->becd', hid, w2) * gate[..., None].astype(h.dtype)
    return jax.vmap(lambda yb, ib: jnp.zeros((T, D), yb.dtype).at[ib.reshape(-1)].add(yb.reshape(-1, D)))(ys, idx)


def setup_inputs(seed: int = 0) -> dict:
    key = jax.random.key(seed)
    keys = iter(jax.random.split(key, 64))

    def nrm(shape, scale):
        return jax.random.normal(next(keys), shape, jnp.float32) * scale

    def gain(shape):
        return 1.0 + nrm(shape, 0.05)

    ne, no = (DEPTH + 1) // 2, DEPTH // 2
    d = D_MODEL
    shift_base = jnp.array([0.25, 0.5, 0.25], jnp.float32)[None, :, None]
    return {
        'x': nrm((BATCH, SEQ, d), 1.0),
        'c': nrm((BATCH, d), 1.0),
        'ctx': nrm((BATCH, CTX_LEN, d), 1.0),
        'c_ctx': nrm((d,), 1.0),
        'ada_w': nrm((DEPTH, d, 6 * d), 0.5 * d ** -0.5),
        'ada_b': nrm((DEPTH, 6 * d), 0.02),
        'norm1_g': gain((DEPTH, d)),
        'norm2_g': gain((DEPTH, d)),
        'ev_w_in': nrm((ne, d, EVEN_IN), d ** -0.5),
        'ev_w_out': nrm((ne, EVEN_OUT, d), EVEN_OUT ** -0.5),
        'mla_q_norm': gain((ne, Q_LORA)),
        'mla_w_uq': nrm((ne, Q_LORA, MLA_HEADS * MLA_QK), Q_LORA ** -0.5),
        'mla_kv_norm': gain((ne, KV_LORA)),
        'mla_w_ukv': nrm((ne, KV_LORA, MLA_HEADS * (MLA_NOPE + MLA_V)), KV_LORA ** -0.5),
        'mla_q_g': gain((ne, MLA_QK)),
        'mla_k_g': gain((ne, MLA_QK)),
        'cv_dw_w': nrm((ne, CONV_K, CONV_CH), CONV_K ** -0.5),
        'cv_dw_b': nrm((ne, CONV_CH), 0.02),
        'cv_ln_g': gain((ne, CONV_CH)),
        'cv_ln_b': nrm((ne, CONV_CH), 0.02),
        'od_w_in': nrm((no, d, ODD_IN), d ** -0.5),
        'od_w_out': nrm((no, ODD_OUT, d), ODD_OUT ** -0.5),
        'rw_shift_w': shift_base + nrm((no, SHIFT_K, RW_IN), 0.1),
        'rw_w0': jax.random.uniform(next(keys), (no, 2, RW_DIM), jnp.float32, -4.0, 1.0),
        'rw_w2': nrm((no, 2, DECAY_LORA, RW_DIM), 0.5 * DECAY_LORA ** -0.5),
        'rw_a0': nrm((no, 2, RW_DIM), 0.5),
        'rw_a2': nrm((no, 2, ICLR_LORA, RW_DIM), 0.5 * ICLR_LORA ** -0.5),
        'rw_g2': nrm((no, GATE_LORA, RW_DIM), GATE_LORA ** -0.5),
        'rw_k_k': 0.85 + nrm((no, RW_DIM), 0.05),
        'rw_k_a': gain((no, RW_DIM)),
        'rw_r_k': nrm((no, RW_DIM), 0.3),
        'rw_gn_g': gain((no, RW_DIM)),
        'rw_gn_b': nrm((no, RW_DIM), 0.02),
        'na_q_g': gain((no, NA_DIM)),
        'na_k_g': gain((no, NA_DIM)),
        'na_rpb': nrm((no, NA_HEADS, 2 * WIN_H - 1, 2 * WIN_W - 1), 0.5),
        'moe_router': nrm((DEPTH, d, N_EXPERTS), d ** -0.5),
        'moe_w1': nrm((DEPTH, N_EXPERTS, d, D_EXPERT), d ** -0.5),
        'moe_w3': nrm((DEPTH, N_EXPERTS, d, D_EXPERT), d ** -0.5),
        'moe_w2': nrm((DEPTH, N_EXPERTS, D_EXPERT, d), D_EXPERT ** -0.5),
    }


def reference(x, c, ctx, c_ctx, ada_w, ada_b, norm1_g, norm2_g,
              ev_w_in, ev_w_out, mla_q_norm, mla_w_uq, mla_kv_norm, mla_w_ukv, mla_q_g, mla_k_g,
              cv_dw_w, cv_dw_b, cv_ln_g, cv_ln_b,
              od_w_in, od_w_out, rw_shift_w, rw_w0, rw_w2, rw_a0, rw_a2, rw_g2, rw_k_k, rw_k_a, rw_r_k,
              rw_gn_g, rw_gn_b, na_q_g, na_k_g, na_rpb,
              moe_router, moe_w1, moe_w3, moe_w2):
    rope_cs = axial_rope_tables(x.shape[1], MLA_ROPE)
    for i in range(DEPTH):
        last = i == DEPTH - 1
        j = i // 2
        mod = jnp.einsum('bd,de->be', jax.nn.silu(c), ada_w[i]) + ada_b[i]
        mod_c = jax.nn.silu(c_ctx) @ ada_w[i] + ada_b[i]
        sh1, sc1, g1, sh2, sc2, g2 = jnp.split(mod[:, None, :], 6, -1)
        csh1, csc1, cg1, csh2, csc2, cg2 = jnp.split(mod_c[None, None, :], 6, -1)
        h = rms_norm(x, norm1_g[i]) * (1.0 + sc1) + sh1
        hc = rms_norm(ctx, norm1_g[i]) * (1.0 + csc1) + csh1
        if i % 2 == 0:
            y, yc = even_mixer(h, hc, rope_cs, ev_w_in[j], ev_w_out[j], mla_q_norm[j], mla_w_uq[j],
                               mla_kv_norm[j], mla_w_ukv[j], mla_q_g[j], mla_k_g[j],
                               cv_dw_w[j], cv_dw_b[j], cv_ln_g[j], cv_ln_b[j], not last)
        else:
            y, yc = odd_mixer(h, hc, od_w_in[j], od_w_out[j], rw_shift_w[j], rw_w0[j], rw_w2[j],
                              rw_a0[j], rw_a2[j], rw_g2[j], rw_k_k[j], rw_k_a[j], rw_r_k[j],
                              rw_gn_g[j], rw_gn_b[j], na_q_g[j], na_k_g[j], na_rpb[j], not last)
        x = x + g1 * y
        h = rms_norm(x, norm2_g[i]) * (1.0 + sc2) + sh2
        x = x + g2 * expert_choice_ffn(h, moe_router[i], moe_w1[i], moe_w3[i], moe_w2[i])
        if not last:
            ctx = ctx + cg1 * yc
            hc = rms_norm(ctx, norm2_g[i]) * (1.0 + csc2) + csh2
            ctx = ctx + cg2 * expert_choice_ffn(hc, moe_router[i], moe_w1[i], moe_w3[i], moe_w2[i])
    return x
```

```python
import functools
import math

import numpy as np
import jax
import jax.numpy as jnp
from jax import lax
from jax.experimental import pallas as pl
from jax.experimental.pallas import tpu as pltpu

f32 = jnp.float32
bf16 = jnp.bfloat16
HIGHEST = lax.Precision.HIGHEST

D_MODEL = 1024
GRID_W = 64
ROPE_BASE = 10000.0
EPS = 1e-6

MLA_HEADS = D_MODEL // 128
MLA_NOPE = 64
MLA_ROPE = 32
MLA_QK = MLA_NOPE + MLA_ROPE
MLA_V = 64
Q_LORA = 3 * D_MODEL // 8
KV_LORA = D_MODEL // 4
MLA_IN = Q_LORA + KV_LORA + MLA_ROPE
MLA_IN_PAD = 768
CONV_CH = D_MODEL // 2
CONV_K = 31

RW_N = 64
RW_HEADS = D_MODEL // 128
RW_DIM = RW_HEADS * RW_N
DECAY_LORA = 64
ICLR_LORA = 64
GATE_LORA = 128
SHIFT_K = 3
RW_IN = 3 * RW_DIM + 2 * DECAY_LORA + 2 * ICLR_LORA + GATE_LORA
GN_EPS = 64e-5

NA_HEADS = D_MODEL // 128
NA_DIM = 64
WIN_H = 8
WIN_W = 16
NA_IN = 3 * NA_HEADS * NA_DIM
NA_QROWS = 4
NA_BAND = 12

N_EXPERTS = 16
EC_FACTOR = 2

WKV_CHUNK = 64
WKV_HEADS_PER_STEP = 8
CONV_HALO = 32
CONV_FRONT = 16
MASK_VALUE = -1e30

VMEM_LIMIT = 56 * 1024 * 1024


def _cparams(sem):
    return pltpu.CompilerParams(dimension_semantics=sem, vmem_limit_bytes=VMEM_LIMIT)


def _mm(a, b):
    return jnp.dot(a.astype(bf16), b.astype(bf16), preferred_element_type=f32)


def _linear_kernel(*refs, norm, has_scale, has_shift, has_x2, has_bias, has_res, emit_x):
    it = iter(refs)
    x_ref, w_ref = next(it), next(it)
    scale_ref = next(it) if has_scale else None
    shift_ref = next(it) if has_shift else None
    x2_ref = next(it) if has_x2 else None
    w2_ref = next(it) if has_x2 else None
    bias_ref = next(it) if has_bias else None
    res_ref = next(it) if has_res else None
    gate_ref = next(it) if has_res else None
    o_ref = next(it)
    xo_ref = next(it) if emit_x else None
    xb_ref = next(it)

    @pl.when(pl.program_id(2) == 0)
    def _():
        x = x_ref[0].astype(f32)
        if norm:
            x = x * lax.rsqrt(jnp.mean(x * x, axis=-1, keepdims=True) + EPS)
        if has_scale:
            x = x * scale_ref[0]
        if has_shift:
            x = x + shift_ref[0]
        xb_ref[...] = x.astype(bf16)
        if emit_x:
            xo_ref[0] = xb_ref[...]

    acc = jnp.dot(xb_ref[...], w_ref[...], preferred_element_type=f32)
    if has_x2:
        acc = acc + jnp.dot(x2_ref[0].astype(bf16), w2_ref[...], preferred_element_type=f32)
    if has_bias:
        acc = acc + bias_ref[...]
    if has_res:
        acc = res_ref[0] + gate_ref[0] * acc
    o_ref[0] = acc.astype(o_ref.dtype)


def _pick_tile(n, target, align):
    if n <= target:
        return n
    t = (target // align) * align
    while t > align and n % t:
        t -= align
    assert n % t == 0, (n, target, align)
    return t


def linear(x, w, *, norm=False, scale=None, shift=None, x2=None, w2=None, bias=None,
           res=None, gate=None, emit_x=False, tm=512, tn=None, out_dtype=f32):
    B, T, K = x.shape
    N = w.shape[1]
    assert w.shape[0] == K and N % 128 == 0, (x.shape, w.shape)
    tm = _pick_tile(T, tm, 8)
    tn = _pick_tile(N, 2048 if tn is None else tn, 128)
    grid = (B, T // tm, N // tn)

    def bvec(a):
        return (lambda b, i, j: (b, 0, 0)) if a.shape[0] > 1 else (lambda b, i, j: (0, 0, 0))

    args = [x, w]
    specs = [pl.BlockSpec((1, tm, K), lambda b, i, j: (b, i, 0)),
             pl.BlockSpec((K, tn), lambda b, i, j: (0, j))]
    if scale is not None:
        args.append(scale.astype(f32))
        specs.append(pl.BlockSpec((1, 1, K), bvec(scale)))
    if shift is not None:
        args.append(shift.astype(f32))
        specs.append(pl.BlockSpec((1, 1, K), bvec(shift)))
    if x2 is not None:
        K2 = x2.shape[-1]
        args += [x2, w2]
        specs += [pl.BlockSpec((1, tm, K2), lambda b, i, j: (b, i, 0)),
                  pl.BlockSpec((K2, tn), lambda b, i, j: (0, j))]
    if bias is not None:
        args.append(bias.astype(f32))
        specs.append(pl.BlockSpec((1, tn), lambda b, i, j: (0, j)))
    if res is not None:
        args += [res, gate.astype(f32)]
        specs += [pl.BlockSpec((1, tm, tn), lambda b, i, j: (b, i, j)),
                  pl.BlockSpec((1, 1, tn), (lambda b, i, j: (b, 0, j)) if gate.shape[0] > 1
                               else (lambda b, i, j: (0, 0, j)))]
    out_shape = [jax.ShapeDtypeStruct((B, T, N), out_dtype)]
    out_specs = [pl.BlockSpec((1, tm, tn), lambda b, i, j: (b, i, j))]
    if emit_x:
        out_shape.append(jax.ShapeDtypeStruct((B, T, K), bf16))
        out_specs.append(pl.BlockSpec((1, tm, K), lambda b, i, j: (b, i, 0)))
    kern = functools.partial(_linear_kernel, norm=norm, has_scale=scale is not None,
                             has_shift=shift is not None, has_x2=x2 is not None,
                             has_bias=bias is not None, has_res=res is not None, emit_x=emit_x)
    outs = pl.pallas_call(
        kern, grid=grid, in_specs=specs, out_specs=out_specs, out_shape=out_shape,
        scratch_shapes=[pltpu.VMEM((tm, K), bf16)],
        compiler_params=_cparams(("parallel", "parallel", "arbitrary")),
        name="linear",
    )(*args)
    return outs if emit_x else outs[0]


def _attn_kernel(q_ref, k_ref, v_ref, o_ref):
    q = q_ref[0, 0]
    s = lax.dot_general(q, k_ref[0, 0], (((1,), (1,)), ((), ())), preferred_element_type=f32)
    m = jnp.max(s, axis=-1, keepdims=True)
    p = jnp.exp(s - m)
    l = jnp.sum(p, axis=-1, keepdims=True)
    o = jnp.dot(p.astype(bf16), v_ref[0, 0], preferred_element_type=f32)
    o_ref[0, 0] = o / l


def attention(q, k, v, *, tq=256):
    B, H, T, Dq = q.shape
    S, Dv = k.shape[2], v.shape[3]
    tq = _pick_tile(T, tq, 8)
    return pl.pallas_call(
        _attn_kernel, grid=(B, H, T // tq),
        in_specs=[pl.BlockSpec((1, 1, tq, Dq), lambda b, h, i: (b, h, i, 0)),
                  pl.BlockSpec((1, 1, S, Dq), lambda b, h, i: (b, h, 0, 0)),
                  pl.BlockSpec((1, 1, S, Dv), lambda b, h, i: (b, h, 0, 0))],
        out_specs=pl.BlockSpec((1, 1, tq, Dv), lambda b, h, i: (b, h, i, 0)),
        out_shape=jax.ShapeDtypeStruct((B, H, T, Dv), f32),
        compiler_params=_cparams(("parallel", "parallel", "arbitrary")),
        name="attention",
    )(q, k, v)


def _dwconv_kernel(*refs, taps, tb, C, glu, post):
    it = iter(refs)
    xm_ref, xh_ref, w_ref = next(it), next(it), next(it)
    if post:
        b_ref, g_ref, bb_ref = next(it), next(it), next(it)
    o_ref = next(it)
    hs_ref = next(it)

    def pre(x):
        if glu:
            return x[:, :C] * jax.nn.sigmoid(x[:, C:])
        return x

    hs_ref[0:tb, :] = pre(xm_ref[0])
    hs_ref[tb:tb + CONV_HALO, :] = pre(xh_ref[0])
    off = CONV_FRONT - (taps - 1) // 2
    acc = jnp.zeros((tb, C), f32)
    for j in range(taps):
        acc = acc + w_ref[j:j + 1, :] * hs_ref[off + j:off + j + tb, :]
    if post:
        acc = acc + b_ref[...]
        mu = jnp.mean(acc, axis=-1, keepdims=True)
        d = acc - mu
        var = jnp.mean(d * d, axis=-1, keepdims=True)
        y = d * lax.rsqrt(var + EPS) * g_ref[...] + bb_ref[...]
        acc = y * jax.nn.sigmoid(y)
    o_ref[0] = acc


def dwconv(x, w, *, glu=False, post=None, tb=256):
    B, T, Cin = x.shape
    taps, C = w.shape
    assert Cin == (2 * C if glu else C)
    tb = _pick_tile(T, tb, CONV_HALO)
    assert tb % CONV_HALO == 0
    xp = jnp.pad(x, ((0, 0), (CONV_FRONT, CONV_HALO - CONV_FRONT), (0, 0)))
    args = [xp, xp, w]
    specs = [pl.BlockSpec((1, tb, Cin), lambda b, i: (b, i, 0)),
             pl.BlockSpec((1, CONV_HALO, Cin), lambda b, i: (b, (i + 1) * (tb // CONV_HALO), 0)),
             pl.BlockSpec((taps, C), lambda b, i: (0, 0))]
    if post is not None:
        args += [p.reshape(1, C) for p in post]
        specs += [pl.BlockSpec((1, C), lambda b, i: (0, 0))] * 3
    kern = functools.partial(_dwconv_kernel, taps=taps, tb=tb, C=C, glu=glu, post=post is not None)
    return pl.pallas_call(
        kern, grid=(B, T // tb), in_specs=specs,
        out_specs=pl.BlockSpec((1, tb, C), lambda b, i: (b, i, 0)),
        out_shape=jax.ShapeDtypeStruct((B, T, C), f32),
        scratch_shapes=[pltpu.VMEM((tb + CONV_HALO, C), f32)],
        compiler_params=_cparams(("parallel", "parallel")),
        name="dwconv",
    )(*args)


def _bmm(a, b):
    return jnp.einsum('gij,gjk->gik', a.astype(bf16), b.astype(bf16), preferred_element_type=f32)


def _bmm_nt(a, b):
    return jnp.einsum('gik,gjk->gij', a.astype(bf16), b.astype(bf16), preferred_element_type=f32)


def _bmm_tn(a, b):
    return lax.dot_general(a.astype(bf16), b.astype(bf16), (((1,), (1,)), ((0,), (0,))),
                           preferred_element_type=f32)


def _wkv_chunk_kernel(r_ref, lw_ref, k_ref, v_ref, a_ref, b_ref, rh_ref, y0_ref, m_ref, n_ref):
    G, Tc, Nk = r_ref.shape
    r, lw, k, v, a, b = (ref[...] for ref in (r_ref, lw_ref, k_ref, v_ref, a_ref, b_ref))
    ti = lax.broadcasted_iota(jnp.int32, (G, Tc, Tc), 1)
    tj = lax.broadcasted_iota(jnp.int32, (G, Tc, Tc), 2)
    incl = tj <= ti
    strict = tj < ti
    L = jnp.einsum('gts,gsk->gtk', incl.astype(f32), lw, precision=HIGHEST, preferred_element_type=f32)
    Ltot = L[:, Tc - 1:Tc, :]
    At = a * jnp.exp(L - lw)
    Rt = r * jnp.exp(L)
    enL = jnp.exp(-L)
    Bt = b * enL
    Kt = k * enL
    eR = jnp.exp(Ltot - L)
    Bb = b * eR
    Kb = k * eR
    Aab = jnp.where(strict, _bmm_nt(At, Bt), 0.0)
    Aak = jnp.where(strict, _bmm_nt(At, Kt), 0.0)
    Arb = jnp.where(incl, _bmm_nt(Rt, Bt), 0.0)
    Ark = jnp.where(incl, _bmm_nt(Rt, Kt), 0.0)
    base = 16
    same = lambda size: jnp.right_shift(ti, int(math.log2(size))) == jnp.right_shift(tj, int(math.log2(size)))
    X = jnp.where(same(base), Aab, 0.0)
    Tm = jnp.where(ti == tj, 1.0, 0.0) + X
    span = 2
    while span < base:
        X = _bmm(X, X)
        Tm = Tm + _bmm(Tm, X)
        span *= 2
    size = base
    while size < Tc:
        off = jnp.where(same(2 * size) & jnp.logical_not(same(size)), Aab, 0.0)
        Tm = Tm + _bmm(_bmm(Tm, off), Tm)
        size *= 2
    U0 = _bmm(Tm, _bmm(Aak, v))
    Ah = _bmm(Tm, At)
    rh_ref[...] = Rt + _bmm(Arb, Ah)
    y0_ref[...] = _bmm(Arb, U0) + _bmm(Ark, v)
    ki = lax.broadcasted_iota(jnp.int32, (G, Nk, Nk), 1)
    kj = lax.broadcasted_iota(jnp.int32, (G, Nk, Nk), 2)
    m_ref[:, 0] = _bmm_tn(Bb, Ah) + jnp.where(ki == kj, jnp.exp(Ltot), 0.0)
    n_ref[:, 0] = _bmm_tn(Bb, U0) + _bmm_tn(Kb, v)


def _wkv_scan_kernel(rh_ref, y0_ref, m_ref, n_ref, y_ref, h_ref):
    @pl.when(pl.program_id(1) == 0)
    def _():
        h_ref[...] = jnp.zeros_like(h_ref)

    H = h_ref[...]
    y_ref[...] = y0_ref[...] + _bmm(rh_ref[...], H)
    h_ref[...] = _bmm(m_ref[:, 0], H) + n_ref[:, 0]


def wkv7(r, lw, k, v, a, b):
    BH, Tt, Nk = r.shape
    Tc, G = WKV_CHUNK, WKV_HEADS_PER_STEP
    assert Tt % Tc == 0 and BH % G == 0
    nc = Tt // Tc
    seq = pl.BlockSpec((G, Tc, Nk), lambda g, c: (g, c, 0))
    mat = pl.BlockSpec((G, 1, Nk, Nk), lambda g, c: (g, c, 0, 0))
    seq_shape = jax.ShapeDtypeStruct((BH, Tt, Nk), f32)
    mat_shape = jax.ShapeDtypeStruct((BH, nc, Nk, Nk), f32)
    rh, y0, m, n = pl.pallas_call(
        _wkv_chunk_kernel, grid=(BH // G, nc), in_specs=[seq] * 6,
        out_specs=[seq, seq, mat, mat], out_shape=[seq_shape, seq_shape, mat_shape, mat_shape],
        compiler_params=_cparams(("parallel", "parallel")),
        name="wkv_chunk",
    )(r, lw, k, v, a, b)
    return pl.pallas_call(
        _wkv_scan_kernel, grid=(BH // G, nc), in_specs=[seq, seq, mat, mat],
        out_specs=seq, out_shape=seq_shape,
        scratch_shapes=[pltpu.VMEM((G, Nk, Nk), f32)],
        compiler_params=_cparams(("parallel", "arbitrary")),
        name="wkv_scan",
    )(rh, y0, m, n)


def _natten_kernel(q_ref, k0_ref, k1_ref, k2_ref, v0_ref, v1_ref, v2_ref, kc_ref, vc_ref, bias_ref, o_ref):
    q = q_ref[0, 0]
    nq = q.shape[0]
    nt = lambda x, y: lax.dot_general(x, y, (((1,), (1,)), ((), ())), preferred_element_type=f32)
    s = [nt(q, kr[0, 0]) + bias_ref[0, 0, :, j * nq:(j + 1) * nq]
         for j, kr in enumerate((k0_ref, k1_ref, k2_ref))]
    s.append(nt(q, kc_ref[0, 0]))
    m = functools.reduce(jnp.maximum, [jnp.max(x, axis=-1, keepdims=True) for x in s])
    p = [jnp.exp(x - m) for x in s]
    l = functools.reduce(jnp.add, [jnp.sum(x, axis=-1, keepdims=True) for x in p])
    vals = (v0_ref, v1_ref, v2_ref, vc_ref)
    o = functools.reduce(jnp.add, [jnp.dot(pp.astype(bf16), vr[0, 0], preferred_element_type=f32)
                                   for pp, vr in zip(p, vals)])
    o_ref[0, 0] = o / l


def _natten_bias_table(rpb, rows):
    W = GRID_W
    kh, kw = min(WIN_H, rows), min(WIN_W, W)
    tabs = []
    for r0, bs in ((0, 0), (NA_QROWS, 0), (rows - NA_QROWS, rows - NA_BAND)):
        r = r0 + np.arange(NA_QROWS)[:, None, None, None]
        c = np.arange(W)[None, :, None, None]
        kr = bs + np.arange(NA_BAND)[None, None, :, None]
        kc = np.arange(W)[None, None, None, :]
        rs = np.clip(r - kh // 2, 0, rows - kh)
        cs = np.clip(c - kw // 2, 0, W - kw)
        valid = (kr >= rs) & (kr < rs + kh) & (kc >= cs) & (kc < cs + kw)
        drow = np.clip(kr - r + (WIN_H - 1), 0, 2 * WIN_H - 2)
        dcol = np.clip(kc - c + (WIN_W - 1), 0, 2 * WIN_W - 2)
        shape = (NA_QROWS, W, NA_BAND, W)
        valid, drow, dcol = (np.broadcast_to(t, shape).reshape(NA_QROWS * W, NA_BAND * W) for t in (valid, drow, dcol))
        tabs.append(jnp.where(valid[None], rpb[:, drow, dcol], MASK_VALUE))
    return jnp.stack(tabs, axis=1)


def natten(q, k, v, kc, vc, rpb):
    B, H, T, Dh = q.shape
    L = kc.shape[2]
    rows = T // GRID_W
    assert rows % NA_QROWS == 0 and rows >= NA_BAND and NA_BAND == 3 * NA_QROWS
    nb = rows // NA_QROWS
    nq = NA_QROWS * GRID_W
    table = _natten_bias_table(rpb.astype(f32), rows)

    def band(j):
        return pl.BlockSpec((1, 1, nq, Dh), lambda b, h, i: (b, h, jnp.clip(i - 1, 0, nb - 3) + j, 0))

    ctx = pl.BlockSpec((1, 1, L, Dh), lambda b, h, i: (b, h, 0, 0))
    cls = lambda b, h, i: (h, jnp.where(i == 0, 0, jnp.where(i == nb - 1, 2, 1)), 0, 0)
    return pl.pallas_call(
        _natten_kernel, grid=(B, H, nb),
        in_specs=[pl.BlockSpec((1, 1, nq, Dh), lambda b, h, i: (b, h, i, 0)),
                  band(0), band(1), band(2), band(0), band(1), band(2), ctx, ctx,
                  pl.BlockSpec((1, 1, nq, 3 * nq), cls)],
        out_specs=pl.BlockSpec((1, 1, nq, Dh), lambda b, h, i: (b, h, i, 0)),
        out_shape=jax.ShapeDtypeStruct((B, H, T, Dh), f32),
        compiler_params=_cparams(("parallel", "parallel", "arbitrary")),
        name="natten",
    )(q, k, k, k, v, v, v, kc, vc, table)


def _moe_ffn_kernel(h_ref, idx_ref, gate_ref, w1_ref, w3_ref, w2_ref, o_ref):
    C = idx_ref.shape[2]
    T = h_ref.shape[1]
    tok = lax.broadcasted_iota(jnp.int32, (C, T), 1)
    onehot = jnp.where(tok == idx_ref[0, 0], 1.0, 0.0).astype(bf16)
    xs = jnp.dot(onehot, h_ref[0], preferred_element_type=f32).astype(bf16)
    a1 = jnp.dot(xs, w1_ref[0], preferred_element_type=f32)
    a3 = jnp.dot(xs, w3_ref[0], preferred_element_type=f32)
    hid = (a1 * jax.nn.sigmoid(a1) * a3).astype(bf16)
    ys = jnp.dot(hid, w2_ref[0], preferred_element_type=f32) * gate_ref[0, 0]
    o_ref[0, 0] = ys.astype(o_ref.dtype)


def moe_ffn(h, idx, gate, w1, w3, w2, *, expert_major):
    B, T, D = h.shape
    E, C = idx.shape[1], idx.shape[2]
    F = w1.shape[2]
    if expert_major:
        grid, be = (E, B), (lambda e, b: (b, e))
    else:
        grid, be = (B, E), (lambda b, e: (b, e))
    bmap = lambda *g: (be(*g)[0], 0, 0)
    emap = lambda *g: (be(*g)[1], 0, 0)
    bemap = lambda *g: (*be(*g), 0, 0)
    return pl.pallas_call(
        _moe_ffn_kernel, grid=grid,
        in_specs=[pl.BlockSpec((1, T, D), bmap),
                  pl.BlockSpec((1, 1, C, 1), bemap),
                  pl.BlockSpec((1, 1, C, 1), bemap),
                  pl.BlockSpec((1, D, F), emap),
                  pl.BlockSpec((1, D, F), emap),
                  pl.BlockSpec((1, F, D), emap)],
        out_specs=pl.BlockSpec((1, 1, C, D), bemap),
        out_shape=jax.ShapeDtypeStruct((B, E, C, D), bf16),
        compiler_params=_cparams(("parallel", "arbitrary")),
        name="moe_ffn",
    )(h, idx[..., None], gate[..., None].astype(f32), w1, w3, w2)


def _moe_scatter_kernel(ys_ref, idx_ref, x_ref, g_ref, o_ref, *, chunk):
    tt = x_ref.shape[1]
    EC = ys_ref.shape[1]
    t0 = pl.program_id(2) * tt
    tok = lax.broadcasted_iota(jnp.int32, (tt, chunk), 0) + t0
    acc = jnp.zeros(o_ref.shape[1:], f32)
    for s in range(EC // chunk):
        onehot = jnp.where(tok == idx_ref[0, :, s * chunk:(s + 1) * chunk], 1.0, 0.0).astype(bf16)
        acc = acc + jnp.dot(onehot, ys_ref[0, s * chunk:(s + 1) * chunk, :], preferred_element_type=f32)
    o_ref[0] = x_ref[0] + g_ref[0] * acc


def moe_scatter(ys, idx, x, g, *, tt=512, dn=512):
    B, E, C, D = ys.shape
    T = x.shape[1]
    EC = E * C
    tt = _pick_tile(T, tt, 8)
    dn = _pick_tile(D, dn, 128)
    chunk = _pick_tile(EC, 512, 128)
    gmap = (lambda b, d, i: (b, 0, d)) if g.shape[0] > 1 else (lambda b, d, i: (0, 0, d))
    return pl.pallas_call(
        functools.partial(_moe_scatter_kernel, chunk=chunk), grid=(B, D // dn, T // tt),
        in_specs=[pl.BlockSpec((1, EC, dn), lambda b, d, i: (b, 0, d)),
                  pl.BlockSpec((1, 1, EC), lambda b, d, i: (b, 0, 0)),
                  pl.BlockSpec((1, tt, dn), lambda b, d, i: (b, i, d)),
                  pl.BlockSpec((1, 1, dn), gmap)],
        out_specs=pl.BlockSpec((1, tt, dn), lambda b, d, i: (b, i, d)),
        out_shape=jax.ShapeDtypeStruct((B, T, D), f32),
        compiler_params=_cparams(("parallel", "parallel", "arbitrary")),
        name="moe_scatter",
    )(ys.reshape(B, EC, D), idx.reshape(B, 1, EC), x, g.astype(f32))


def expert_choice_ffn(x, scale, shift, g, router, w1, w3, w2, *, expert_major):
    B, T, D = x.shape
    E = router.shape[1]
    cap = max(1, EC_FACTOR * T // E)
    router_p = jnp.pad(router, ((0, 0), (0, 128 - E))).astype(bf16)
    logits, h = linear(x, router_p, norm=True, scale=scale, shift=shift, emit_x=True)
    aff = jax.nn.softmax(logits[..., :E], axis=-1)
    gate, idx = lax.top_k(jnp.swapaxes(aff, 1, 2), cap)
    ys = moe_ffn(h, idx, gate, w1, w3, w2, expert_major=expert_major)
    return moe_scatter(ys, idx, x, g)


def _rms(x, g):
    return x * lax.rsqrt(jnp.mean(jnp.square(x), -1, keepdims=True) + EPS) * g


def _rope_tables(n_tok, rot_dim):
    t = jnp.arange(n_tok, dtype=jnp.int32)
    row = (t // GRID_W).astype(f32)
    col = (t % GRID_W).astype(f32)
    half = rot_dim // 2
    inv = ROPE_BASE ** (-jnp.arange(0, half, 2, dtype=f32) / half)
    ar = row[:, None] * inv
    ac = col[:, None] * inv
    ang = jnp.concatenate([ar, ar, ac, ac], -1)
    return jnp.cos(ang), jnp.sin(ang)


def _rotate_half(u):
    u1, u2 = jnp.split(u, 2, -1)
    return jnp.concatenate([-u2, u1], -1)


def _rope(x, cos, sin):
    half = x.shape[-1] // 2
    rot = jnp.concatenate([_rotate_half(x[..., :half]), _rotate_half(x[..., half:])], -1)
    return x * cos[:, None, :] + rot * sin[:, None, :]


def _heads_first(x):
    return jnp.swapaxes(x, 1, 2)


def _mla_qkv(u, p, rope_cs):
    B, T, _ = u.shape
    one = lambda g: g.reshape(1, 1, -1)
    q = linear(u[..., :Q_LORA], p['w_uq'], norm=True, scale=one(p['q_norm']))
    kv = linear(u[..., Q_LORA:Q_LORA + KV_LORA], p['w_ukv'], norm=True, scale=one(p['kv_norm']))
    kr = u[..., Q_LORA + KV_LORA:MLA_IN]
    q = q.reshape(B, T, MLA_HEADS, MLA_QK)
    kv = kv.reshape(B, T, MLA_HEADS, MLA_NOPE + MLA_V)
    k_nope, v = kv[..., :MLA_NOPE], kv[..., MLA_NOPE:]
    k = jnp.concatenate([k_nope, jnp.broadcast_to(kr[:, :, None, :], (B, T, MLA_HEADS, MLA_ROPE))], -1)
    q = _rms(q, p['q_g'])
    k = _rms(k, p['k_g'])
    if rope_cs is not None:
        cos, sin = rope_cs
        q = jnp.concatenate([q[..., :MLA_NOPE], _rope(q[..., MLA_NOPE:], cos, sin)], -1)
        k = jnp.concatenate([k[..., :MLA_NOPE], _rope(k[..., MLA_NOPE:], cos, sin)], -1)
    q = q * (MLA_QK ** -0.5)
    return tuple(_heads_first(t).astype(bf16) for t in (q, k, v))


def _merge_heads(o):
    B, H, T, Dh = o.shape
    return jnp.swapaxes(o, 1, 2).reshape(B, T, H * Dh)


def even_mixer(x, ctx, mods, p, rope_cs, with_ctx_out):
    (scale, shift, gate), (cscale, cshift, cgate) = mods
    u = linear(x, p['w_in'], norm=True, scale=scale, shift=shift)
    uc = linear(ctx, p['w_in'], norm=True, scale=cscale, shift=cshift)
    q, k, v = _mla_qkv(u[..., :MLA_IN_PAD], p, rope_cs)
    qc, kc, vc = _mla_qkv(uc[..., :MLA_IN_PAD], p, None)
    o_att = _merge_heads(attention(q, jnp.concatenate([k, kc], 2), jnp.concatenate([v, vc], 2)))
    post = (p['dw_b'], p['ln_g'], p['ln_b'])
    o_conv = dwconv(u[..., MLA_IN_PAD:], p['dw_w'], glu=True, post=post)
    half = MLA_HEADS * MLA_V
    out = lambda res, g, oa, oc: linear(oa, p['w_out'][:half], x2=oc, w2=p['w_out'][half:], res=res, gate=g)
    x_new = out(x, gate, o_att, o_conv)
    if not with_ctx_out:
        return x_new, None
    oc_att = _merge_heads(attention(qc, kc, vc))
    oc_conv = dwconv(uc[..., MLA_IN_PAD:], p['dw_w'], glu=True, post=post)
    return x_new, out(ctx, cgate, oc_att, oc_conv)


def _rwkv_inputs(u, p):
    B, T, _ = u.shape
    o1 = 3 * RW_DIM
    o2 = o1 + 2 * DECAY_LORA
    o3 = o2 + 2 * ICLR_LORA
    r, k, v = jnp.split(u[..., :o1], 3, axis=-1)
    lw = u[..., o1:o2].reshape(B, T, 2, DECAY_LORA)
    la = u[..., o2:o3].reshape(B, T, 2, ICLR_LORA)
    lg = u[..., o3:]
    hd = lambda t: t.reshape(B, T, RW_HEADS, RW_N)
    kk = hd(k * p['k_k'])
    kk = kk * lax.rsqrt(jnp.sum(kk * kk, -1, keepdims=True) + 1e-12)
    g = linear(jax.nn.sigmoid(lg), p['g2'])
    dirs = []
    for d in range(2):
        logw = linear(jnp.tanh(lw[:, :, d]), p['w2'][d], bias=p['w0'][d:d + 1])
        logdecay = -jnp.exp(-jax.nn.softplus(-logw) - 0.5)
        a = jax.nn.sigmoid(linear(la[:, :, d], p['a2'][d], bias=p['a0'][d:d + 1]))
        k_eff = k * (1.0 + (a - 1.0) * p['k_a'])
        dirs.append((hd(logdecay), hd(k_eff), -kk, kk * hd(a)))
    return hd(r), hd(v), dirs, g


def rwkv_mixer(u, uc, p, with_ctx_out):
    B, T, _ = u.shape
    L = uc.shape[1]
    lat = _rwkv_inputs(dwconv(u, p['shift_w']), p)
    cx = _rwkv_inputs(dwconv(uc, p['shift_w']), p)
    rk = p['r_k'].reshape(RW_HEADS, RW_N)

    def seq(tc, tl, d):
        if d == 1:
            tc, tl = jnp.flip(tc, 1), jnp.flip(tl, 1)
        s = jnp.concatenate([tc, tl], 1)
        return _heads_first(s).reshape(B * RW_HEADS, L + T, RW_N)

    streams = [[], [], [], [], [], []]
    for d in range(2):
        (lw_c, k_c, a_c, b_c), (lw_l, k_l, a_l, b_l) = cx[2][d], lat[2][d]
        for lst, tc, tl in zip(streams, (cx[0], lw_c, k_c, cx[1], a_c, b_c), (lat[0], lw_l, k_l, lat[1], a_l, b_l)):
            lst.append(seq(tc, tl, d))
    y = wkv7(*(jnp.concatenate(s, 0) for s in streams))
    y = y.reshape(2, B, RW_HEADS, L + T, RW_N)
    y_bwd = jnp.concatenate([jnp.flip(y[1][:, :, :L], 2), jnp.flip(y[1][:, :, L:], 2)], 2)
    y = jnp.swapaxes(y[0] + y_bwd, 1, 2)

    def finish(inp, y_part):
        r, v, dirs, g = inp
        bonus = sum(jnp.sum(r * dirs[d][1] * rk, -1, keepdims=True) for d in range(2)) * v
        yy = y_part + bonus
        mu = jnp.mean(yy, -1, keepdims=True)
        var = jnp.mean(jnp.square(yy - mu), -1, keepdims=True)
        yn = ((yy - mu) * lax.rsqrt(var + GN_EPS)).reshape(yy.shape[0], yy.shape[1], RW_DIM)
        return (yn * p['gn_g'] + p['gn_b']) * g

    y_lat = finish(lat, y[:, L:])
    return y_lat, (finish(cx, y[:, :L]) if with_ctx_out else None)


def _natten_qkv(u, p):
    B, T, _ = u.shape
    q, k, v = [t.reshape(B, T, NA_HEADS, NA_DIM) for t in jnp.split(u, 3, -1)]
    q = _rms(q, p['na_q_g']) * (NA_DIM ** -0.5)
    k = _rms(k, p['na_k_g'])
    return tuple(_heads_first(t).astype(bf16) for t in (q, k, v))


def odd_mixer(x, ctx, mods, p, with_ctx_out):
    (scale, shift, gate), (cscale, cshift, cgate) = mods
    u = linear(x, p['w_in'], norm=True, scale=scale, shift=shift)
    uc = linear(ctx, p['w_in'], norm=True, scale=cscale, shift=cshift)
    y_rw, yc_rw = rwkv_mixer(u[..., :RW_IN], uc[..., :RW_IN], p, with_ctx_out)
    q, k, v = _natten_qkv(u[..., RW_IN:], p)
    qc, kc, vc = _natten_qkv(uc[..., RW_IN:], p)
    y_na = _merge_heads(natten(q, k, v, kc, vc, p['rpb']))
    out = lambda res, g, a, b: linear(a, p['w_out'][:RW_DIM], x2=b, w2=p['w_out'][RW_DIM:], res=res, gate=g)
    x_new = out(x, gate, y_rw, y_na)
    if not with_ctx_out:
        return x_new, None
    yc_na = _merge_heads(attention(qc, kc, vc))
    return x_new, out(ctx, cgate, yc_rw, yc_na)


def kernel(x, c, ctx, c_ctx, ada_w, ada_b, norm1_g, norm2_g, ev_w_in, ev_w_out, mla_q_norm, mla_w_uq, mla_kv_norm, mla_w_ukv, mla_q_g, mla_k_g, cv_dw_w, cv_dw_b, cv_ln_g, cv_ln_b, od_w_in, od_w_out, rw_shift_w, rw_w0, rw_w2, rw_a0, rw_a2, rw_g2, rw_k_k, rw_k_a, rw_r_k, rw_gn_g, rw_gn_b, na_q_g, na_k_g, na_rpb, moe_router, moe_w1, moe_w3, moe_w2):
    B, T, D = x.shape
    depth = ada_w.shape[0]
    rope_cs = _rope_tables(T, MLA_ROPE)
    cond = jax.nn.silu(jnp.concatenate([c, c_ctx[None]], 0))
    cond = jnp.pad(cond, ((0, (-(B + 1)) % 8), (0, 0)))[None]
    for i in range(depth):
        last = i == depth - 1
        j = i // 2
        mod = linear(cond, ada_w[i].astype(bf16), bias=ada_b[i][None])[0]
        sh1, sc1, g1, sh2, sc2, g2 = (t[:, None, :] for t in jnp.split(mod[:B], 6, -1))
        csh1, csc1, cg1, csh2, csc2, cg2 = (t[None] for t in jnp.split(mod[B:B + 1], 6, -1))
        n1, n2 = norm1_g[i], norm2_g[i]
        mods = ((n1 * (1.0 + sc1), sh1, g1), (n1 * (1.0 + csc1), csh1, cg1))
        if i % 2 == 0:
            w_in = ev_w_in[j]
            w_in = jnp.concatenate([jnp.pad(w_in[:, :MLA_IN], ((0, 0), (0, MLA_IN_PAD - MLA_IN))), w_in[:, MLA_IN:]], 1)
            p = dict(w_in=w_in.astype(bf16), w_out=ev_w_out[j].astype(bf16), q_norm=mla_q_norm[j],
                     w_uq=mla_w_uq[j].astype(bf16), kv_norm=mla_kv_norm[j], w_ukv=mla_w_ukv[j].astype(bf16),
                     q_g=mla_q_g[j], k_g=mla_k_g[j], dw_w=cv_dw_w[j], dw_b=cv_dw_b[j], ln_g=cv_ln_g[j], ln_b=cv_ln_b[j])
            x, ctx_mix = even_mixer(x, ctx, mods, p, rope_cs, not last)
        else:
            p = dict(w_in=od_w_in[j].astype(bf16), w_out=od_w_out[j].astype(bf16), shift_w=rw_shift_w[j],
                     w0=rw_w0[j], w2=rw_w2[j].astype(bf16), a0=rw_a0[j], a2=rw_a2[j].astype(bf16),
                     g2=rw_g2[j].astype(bf16), k_k=rw_k_k[j], k_a=rw_k_a[j], r_k=rw_r_k[j],
                     gn_g=rw_gn_g[j], gn_b=rw_gn_b[j], na_q_g=na_q_g[j], na_k_g=na_k_g[j], rpb=na_rpb[j])
            x, ctx_mix = odd_mixer(x, ctx, mods, p, not last)
        w1, w3, w2 = (w[i].astype(bf16) for w in (moe_w1, moe_w3, moe_w2))
        x = expert_choice_ffn(x, n2 * (1.0 + sc2), sh2, g2, moe_router[i], w1, w3, w2, expert_major=False)
        if not last:
            ctx = expert_choice_ffn(ctx_mix, n2 * (1.0 + csc2), csh2, cg2, moe_router[i], w1, w3, w2, expert_major=True)
    return x
```

```python
import functools
import math

import numpy as np
import jax
import jax.numpy as jnp
from jax import lax
from jax.experimental import pallas as pl
from jax.experimental.pallas import tpu as pltpu

f32 = jnp.float32
bf16 = jnp.bfloat16
HIGHEST = lax.Precision.HIGHEST

D_MODEL = 1024
GRID_W = 64
ROPE_BASE = 10000.0
EPS = 1e-6
LANES = 128

MLA_HEADS = D_MODEL // 128
MLA_NOPE = 64
MLA_ROPE = 32
MLA_QK = MLA_NOPE + MLA_ROPE
MLA_V = 64
Q_LORA = 3 * D_MODEL // 8
KV_LORA = D_MODEL // 4
MLA_IN = Q_LORA + KV_LORA + MLA_ROPE
MLA_IN_PAD = 768
CONV_CH = D_MODEL // 2
CONV_K = 31

RW_N = 64
RW_HEADS = D_MODEL // 128
RW_DIM = RW_HEADS * RW_N
DECAY_LORA = 64
ICLR_LORA = 64
GATE_LORA = 128
SHIFT_K = 3
RW_IN = 3 * RW_DIM + 2 * DECAY_LORA + 2 * ICLR_LORA + GATE_LORA
GN_EPS = 64e-5

NA_HEADS = D_MODEL // 128
NA_DIM = 64
WIN_H = 8
WIN_W = 16
NA_IN = 3 * NA_HEADS * NA_DIM
NA_QROWS = 4
NA_BAND = 12

N_EXPERTS = 16
EC_FACTOR = 2

WKV_CHUNK = 64
WKV_INV_BASE = 16
CONV_HALO = 32
MASK_VALUE = -1e30

VMEM_LIMIT = 56 * 1024 * 1024


def _cparams(sem):
    return pltpu.CompilerParams(dimension_semantics=sem, vmem_limit_bytes=VMEM_LIMIT)


def _mm(a, b):
    return jnp.dot(a.astype(bf16), b.astype(bf16), preferred_element_type=f32)


def _mm_nt(a, b):
    return lax.dot_general(a.astype(bf16), b.astype(bf16), (((1,), (1,)), ((), ())), preferred_element_type=f32)


def _mm_tn(a, b):
    return lax.dot_general(a.astype(bf16), b.astype(bf16), (((0,), (0,)), ((), ())), preferred_element_type=f32)


def _head_masks():
    lane = lax.broadcasted_iota(jnp.int32, (1, LANES), 1)
    m0 = jnp.where(lane < LANES // 2, 1.0, 0.0).astype(f32)
    return m0, 1.0 - m0


def _head_sum(x, masks):
    s0 = jnp.sum(x * masks[0], axis=-1, keepdims=True)
    s1 = jnp.sum(x * masks[1], axis=-1, keepdims=True)
    return s0 * masks[0] + s1 * masks[1]


def _linear_kernel(*refs, norm, has_scale, has_shift, has_x2, has_bias, has_res, emit_x, splits):
    it = iter(refs)
    x_ref, w_ref = next(it), next(it)
    scale_ref = next(it) if has_scale else None
    shift_ref = next(it) if has_shift else None
    x2_ref = next(it) if has_x2 else None
    w2_ref = next(it) if has_x2 else None
    bias_ref = next(it) if has_bias else None
    res_ref = next(it) if has_res else None
    gate_ref = next(it) if has_res else None
    o_refs = [next(it) for _ in splits]
    xo_ref = next(it) if emit_x else None
    xb_ref = next(it)

    @pl.when(pl.program_id(2) == 0)
    def _():
        x = x_ref[0].astype(f32)
        if norm:
            x = x * lax.rsqrt(jnp.mean(x * x, axis=-1, keepdims=True) + EPS)
        if has_scale:
            x = x * scale_ref[0]
        if has_shift:
            x = x + shift_ref[0]
        xb_ref[...] = x.astype(bf16)
        if emit_x:
            xo_ref[0] = xb_ref[...]

    lo = 0
    for o_ref, width in zip(o_refs, splits):
        cols = slice(lo, lo + width) if len(splits) > 1 else slice(None)
        acc = jnp.dot(xb_ref[...], w_ref[:, cols], preferred_element_type=f32)
        if has_x2:
            acc = acc + jnp.dot(x2_ref[0].astype(bf16), w2_ref[:, cols], preferred_element_type=f32)
        if has_bias:
            acc = acc + bias_ref[:, cols]
        if has_res:
            acc = res_ref[0] + gate_ref[0] * acc
        o_ref[0] = acc.astype(o_ref.dtype)
        lo += width


def _pick_tile(n, target, align):
    if n <= target:
        return n
    t = (target // align) * align
    while t > align and n % t:
        t -= align
    assert n % t == 0, (n, target, align)
    return t


def linear(x, w, *, norm=False, scale=None, shift=None, x2=None, w2=None, bias=None,
           res=None, gate=None, emit_x=False, splits=None, tm=512, tn=None, out_dtype=f32):
    B, T, K = x.shape
    N = w.shape[1]
    assert w.shape[0] == K and N % LANES == 0, (x.shape, w.shape)
    tm = _pick_tile(T, tm, 8)
    if splits is None:
        tn = _pick_tile(N, 2048 if tn is None else tn, LANES)
        widths = (tn,)
    else:
        assert sum(splits) == N and all(s % LANES == 0 for s in splits) and res is None
        tn, widths = N, tuple(splits)
    grid = (B, T // tm, N // tn)

    def bvec(a):
        return (lambda b, i, j: (b, 0, 0)) if a.shape[0] > 1 else (lambda b, i, j: (0, 0, 0))

    args = [x, w]
    specs = [pl.BlockSpec((1, tm, K), lambda b, i, j: (b, i, 0)),
             pl.BlockSpec((K, tn), lambda b, i, j: (0, j))]
    if scale is not None:
        args.append(scale.astype(f32))
        specs.append(pl.BlockSpec((1, 1, K), bvec(scale)))
    if shift is not None:
        args.append(shift.astype(f32))
        specs.append(pl.BlockSpec((1, 1, K), bvec(shift)))
    if x2 is not None:
        K2 = x2.shape[-1]
        args += [x2, w2]
        specs += [pl.BlockSpec((1, tm, K2), lambda b, i, j: (b, i, 0)),
                  pl.BlockSpec((K2, tn), lambda b, i, j: (0, j))]
    if bias is not None:
        args.append(bias.astype(f32))
        specs.append(pl.BlockSpec((1, tn), lambda b, i, j: (0, j)))
    if res is not None:
        args += [res, gate.astype(f32)]
        specs += [pl.BlockSpec((1, tm, tn), lambda b, i, j: (b, i, j)),
                  pl.BlockSpec((1, 1, tn), (lambda b, i, j: (b, 0, j)) if gate.shape[0] > 1
                               else (lambda b, i, j: (0, 0, j)))]
    if splits is None:
        out_shape = [jax.ShapeDtypeStruct((B, T, N), out_dtype)]
        out_specs = [pl.BlockSpec((1, tm, tn), lambda b, i, j: (b, i, j))]
    else:
        out_shape = [jax.ShapeDtypeStruct((B, T, s), out_dtype) for s in splits]
        out_specs = [pl.BlockSpec((1, tm, s), lambda b, i, j: (b, i, 0)) for s in splits]
    if emit_x:
        out_shape.append(jax.ShapeDtypeStruct((B, T, K), bf16))
        out_specs.append(pl.BlockSpec((1, tm, K), lambda b, i, j: (b, i, 0)))
    kern = functools.partial(_linear_kernel, norm=norm, has_scale=scale is not None,
                             has_shift=shift is not None, has_x2=x2 is not None,
                             has_bias=bias is not None, has_res=res is not None, emit_x=emit_x, splits=widths)
    outs = pl.pallas_call(
        kern, grid=grid, in_specs=specs, out_specs=out_specs, out_shape=out_shape,
        scratch_shapes=[pltpu.VMEM((tm, K), bf16)],
        compiler_params=_cparams(("parallel", "parallel", "arbitrary")),
        name="linear",
    )(*args)
    return outs if (emit_x or splits is not None) else outs[0]


def _attn_kernel(q_ref, k_ref, v_ref, o_ref):
    Dv = o_ref.shape[3]
    q = q_ref[0, 0]
    s = lax.dot_general(q, k_ref[0, 0], (((1,), (1,)), ((), ())), preferred_element_type=f32)
    m = jnp.max(s, axis=-1, keepdims=True)
    p = jnp.exp((s - m).astype(bf16))
    o = jnp.dot(p, v_ref[0, 0], preferred_element_type=f32)
    o_ref[0, 0] = o[:, :Dv] / o[:, Dv:Dv + 1]


def attention(q, k, v, *, tq=256):
    B, H, T, Dq = q.shape
    S, Dv = k.shape[2], v.shape[3]
    assert Dv < LANES
    tq = _pick_tile(T, tq, 8)
    v = jnp.concatenate([v, jnp.ones((B, H, S, 1), v.dtype), jnp.zeros((B, H, S, LANES - Dv - 1), v.dtype)], -1)
    return pl.pallas_call(
        _attn_kernel, grid=(B, H, T // tq),
        in_specs=[pl.BlockSpec((1, 1, tq, Dq), lambda b, h, i: (b, h, i, 0)),
                  pl.BlockSpec((1, 1, S, Dq), lambda b, h, i: (b, h, 0, 0)),
                  pl.BlockSpec((1, 1, S, LANES), lambda b, h, i: (b, h, 0, 0))],
        out_specs=pl.BlockSpec((1, 1, tq, Dv), lambda b, h, i: (b, h, i, 0)),
        out_shape=jax.ShapeDtypeStruct((B, H, T, Dv), f32),
        compiler_params=_cparams(("parallel", "parallel", "arbitrary")),
        name="attention",
    )(q, k, v)


def _dwconv_kernel(*refs, taps, tb, C, glu, post):
    it = iter(refs)
    xp_ref, xm_ref, xn_ref, w_ref = next(it), next(it), next(it), next(it)
    if post:
        b_ref, g_ref, bb_ref = next(it), next(it), next(it)
    o_ref = next(it)
    hs_ref = next(it)
    i = pl.program_id(1)

    def pre(x):
        if glu:
            return x[:, :C] * jax.nn.sigmoid(x[:, C:])
        return x

    hs_ref[0:CONV_HALO, :] = pre(xp_ref[0]) * jnp.where(i > 0, 1.0, 0.0)
    hs_ref[CONV_HALO:CONV_HALO + tb, :] = pre(xm_ref[0])
    hs_ref[CONV_HALO + tb:, :] = pre(xn_ref[0]) * jnp.where(i < pl.num_programs(1) - 1, 1.0, 0.0)
    off = CONV_HALO - (taps - 1) // 2
    acc = jnp.zeros((tb, C), f32)
    for j in range(taps):
        acc = acc + w_ref[j:j + 1, :] * hs_ref[off + j:off + j + tb, :]
    if post:
        acc = acc + b_ref[...]
        mu = jnp.mean(acc, axis=-1, keepdims=True)
        d = acc - mu
        var = jnp.mean(d * d, axis=-1, keepdims=True)
        y = d * lax.rsqrt(var + EPS) * g_ref[...] + bb_ref[...]
        acc = y * jax.nn.sigmoid(y)
    o_ref[0] = acc


def dwconv(x, w, *, glu=False, post=None, tb=256):
    B, T, Cin = x.shape
    taps, C = w.shape
    assert Cin == (2 * C if glu else C) and (taps - 1) // 2 <= CONV_HALO
    tb = _pick_tile(T, tb, CONV_HALO)
    assert tb % CONV_HALO == 0
    r, nh = tb // CONV_HALO, T // CONV_HALO
    args = [x, x, x, w]
    specs = [pl.BlockSpec((1, CONV_HALO, Cin), lambda b, i: (b, jnp.maximum(i * r - 1, 0), 0)),
             pl.BlockSpec((1, tb, Cin), lambda b, i: (b, i, 0)),
             pl.BlockSpec((1, CONV_HALO, Cin), lambda b, i: (b, jnp.minimum((i + 1) * r, nh - 1), 0)),
             pl.BlockSpec((taps, C), lambda b, i: (0, 0))]
    if post is not None:
        args += [p.reshape(1, C) for p in post]
        specs += [pl.BlockSpec((1, C), lambda b, i: (0, 0))] * 3
    kern = functools.partial(_dwconv_kernel, taps=taps, tb=tb, C=C, glu=glu, post=post is not None)
    return pl.pallas_call(
        kern, grid=(B, T // tb), in_specs=specs,
        out_specs=pl.BlockSpec((1, tb, C), lambda b, i: (b, i, 0)),
        out_shape=jax.ShapeDtypeStruct((B, T, C), f32),
        scratch_shapes=[pltpu.VMEM((tb + 2 * CONV_HALO, C), f32)],
        compiler_params=_cparams(("parallel", "parallel")),
        name="dwconv",
    )(*args)


def _bmm(a, b):
    return jnp.einsum('gij,gjk->gik', a.astype(bf16), b.astype(bf16), preferred_element_type=f32)


def _bmm_nt(a, b):
    return jnp.einsum('gik,gjk->gij', a.astype(bf16), b.astype(bf16), preferred_element_type=f32)


def _bmm_tn(a, b):
    return lax.dot_general(a.astype(bf16), b.astype(bf16), (((1,), (1,)), ((0,), (0,))),
                           preferred_element_type=f32)


def _wkv_units(r, k_eff, v, lw, a, b, sgn, masks):
    G, Tc, _ = r.shape
    ti = lax.broadcasted_iota(jnp.int32, (G, Tc, Tc), 1)
    tj = lax.broadcasted_iota(jnp.int32, (G, Tc, Tc), 2)
    tri = jnp.where((tj - ti) * sgn <= 0, 1.0, 0.0).astype(bf16)
    hi = lw.astype(bf16)
    rest = lw - hi.astype(f32)
    mid = rest.astype(bf16)
    lo = (rest - mid.astype(f32)).astype(bf16)
    L = _bmm(tri, hi) + _bmm(tri, mid) + _bmm(tri, lo)
    Ltot = jnp.sum(lw, axis=1, keepdims=True)
    enL = jnp.exp(-L)
    eR = jnp.exp(Ltot - L)
    At, Rt, Bt, Kt, Bb, Kb = a * jnp.exp(L - lw), r * jnp.exp(L), b * enL, k_eff * enL, b * eR, k_eff * eR

    stack = lambda x: jnp.concatenate([x * masks[0], x * masks[1]], axis=1)
    Y, X, Xk, Vs = stack(At), stack(Bt), stack(Kt), stack(v)
    n2 = 2 * Tc
    gi = lax.broadcasted_iota(jnp.int32, (G, n2, n2), 1)
    gj = lax.broadcasted_iota(jnp.int32, (G, n2, n2), 2)
    before = ((gj & (Tc - 1)) - (gi & (Tc - 1))) * sgn < 0
    Aab = jnp.where(before, _bmm_nt(Y, X), 0.0)
    Aak = jnp.where(before, _bmm_nt(Y, Xk), 0.0)
    ri = lax.broadcasted_iota(jnp.int32, (G, Tc, 2 * n2), 1)
    rj = lax.broadcasted_iota(jnp.int32, (G, Tc, 2 * n2), 2)
    upto = ((rj & (Tc - 1)) - ri) * sgn <= 0
    RB = jnp.where(upto, _bmm_nt(Rt, jnp.concatenate([X, Xk], axis=1)), 0.0)

    same = lambda size: jnp.right_shift(gi, int(math.log2(size))) == jnp.right_shift(gj, int(math.log2(size)))
    Xp = jnp.where(same(WKV_INV_BASE), Aab, 0.0)
    Tm = jnp.where(gi == gj, 1.0, 0.0) + Xp
    span = 2
    while span < WKV_INV_BASE:
        Xp = _bmm(Xp, Xp)
        Tm = Tm + _bmm(Tm, Xp)
        span *= 2
    size = WKV_INV_BASE
    while size < Tc:
        off = jnp.where(same(2 * size) & jnp.logical_not(same(size)), Aab, 0.0)
        Tm = Tm + _bmm(_bmm(Tm, off), Tm)
        size *= 2

    TA = _bmm(Tm, jnp.concatenate([_bmm(Aak, Vs), Y], axis=2))
    U0, Ah = TA[:, :, :LANES], TA[:, :, LANES:]
    UV = jnp.concatenate([U0, Vs], axis=1)
    Bs = stack(Bb)
    ki = lax.broadcasted_iota(jnp.int32, (G, LANES, LANES), 1)
    kj = lax.broadcasted_iota(jnp.int32, (G, LANES, LANES), 2)
    M = _bmm_tn(Bs, Ah) + jnp.where(ki == kj, jnp.exp(Ltot), 0.0)
    N = _bmm_tn(jnp.concatenate([Bs, stack(Kb)], axis=1), UV)
    return Rt + _bmm(RB[:, :, :n2], Ah), _bmm(RB, UV), M, N


def _wkv_chunk_kernel(r_ref, k_ref, v_ref, lw_ref, la_ref, w2_ref, w0_ref, a2_ref, a0_ref,
                      kk_ref, ka_ref, rk_ref, rh_ref, y0_ref, m_ref, n_ref):
    masks = _head_masks()
    P = r_ref.shape[2] // LANES
    pairs = lambda x: jnp.stack([x[:, q * LANES:(q + 1) * LANES] for q in range(P)], axis=0)
    lw_in, la_in = jnp.tanh(lw_ref[0]), la_ref[0]
    r, k, v = pairs(r_ref[0]), pairs(k_ref[0]), pairs(v_ref[0])
    kk = k * pairs(kk_ref[...])
    kk = kk * lax.rsqrt(_head_sum(kk * kk, masks) + 1e-12)
    lw, k_eff, b, bonus = [], [], [], []
    for d in range(2):
        logw = w0_ref[d] + _mm(lw_in, w2_ref[d])
        lw.append(pairs(-jnp.exp(-jax.nn.softplus(-logw) - 0.5)))
        rate = pairs(jax.nn.sigmoid(a0_ref[d] + _mm(la_in, a2_ref[d])))
        k_eff.append(k * (1.0 + (rate - 1.0) * pairs(ka_ref[...])))
        b.append(kk * rate)
        bonus.append(_head_sum(r * k_eff[d] * pairs(rk_ref[...]), masks) * v)
    both = lambda x: jnp.concatenate([x, x], axis=0)
    cat = lambda xs: jnp.concatenate(xs, axis=0)
    unit = lax.broadcasted_iota(jnp.int32, (2 * P, 1, 1), 0)
    sgn = jnp.where(unit < P, 1, -1)
    rh, y0, M, N = _wkv_units(both(r), cat(k_eff), both(v), cat(lw), both(-kk), cat(b), sgn, masks)
    y0 = y0 + cat(bonus)
    for d in range(2):
        for q in range(P):
            cols = slice(q * LANES, (q + 1) * LANES)
            rh_ref[d, 0, :, cols] = rh[d * P + q].astype(rh_ref.dtype)
            y0_ref[d, 0, :, cols] = y0[d * P + q]
            m_ref[d, 0, 0, q] = M[d * P + q].astype(m_ref.dtype)
            n_ref[d, 0, 0, q] = N[d * P + q]


def wkv_chunk(us, p):
    B, T, _ = us.shape
    Tc = WKV_CHUNK
    assert T % Tc == 0 and RW_IN % RW_DIM == 3 * LANES
    nc, P = T // Tc, RW_DIM // LANES
    wide = lambda i: pl.BlockSpec((1, Tc, RW_DIM), lambda b, c: (b, c, i))
    lora_in = lambda i: pl.BlockSpec((1, Tc, LANES), lambda b, c: (b, c, 3 * P + i))
    full = lambda a: pl.BlockSpec(a.shape, lambda b, c: (0,) * a.ndim)
    seq = pl.BlockSpec((2, 1, Tc, RW_DIM), lambda b, c: (0, b, c, 0))
    mat = pl.BlockSpec((2, 1, 1, P, LANES, LANES), lambda b, c: (0, b, c, 0, 0, 0))
    params = [p['w2cat'], p['w0'][:, None, :], p['a2cat'], p['a0'][:, None, :],
              p['k_k'][None], p['k_a'][None], p['r_k'][None]]
    return pl.pallas_call(
        _wkv_chunk_kernel, grid=(B, nc),
        in_specs=[wide(0), wide(1), wide(2), lora_in(0), lora_in(1)] + [full(a) for a in params],
        out_specs=[seq, seq, mat, mat],
        out_shape=[jax.ShapeDtypeStruct((2, B, T, RW_DIM), bf16), jax.ShapeDtypeStruct((2, B, T, RW_DIM), f32),
                   jax.ShapeDtypeStruct((2, B, nc, P, LANES, LANES), bf16),
                   jax.ShapeDtypeStruct((2, B, nc, P, LANES, LANES), f32)],
        compiler_params=_cparams(("parallel", "parallel")),
        name="wkv_chunk",
    )(us, us, us, us, us, *params)


def _wkv_scan_kernel(*refs, final):
    it = iter(refs)
    rh_ref, y0_ref, m_ref, n_ref, h0_ref = (next(it) for _ in range(5))
    if final:
        yo_ref, lg_ref, g2_ref, gg_ref, gb_ref = (next(it) for _ in range(5))
    y_ref, ht_ref, h_ref = next(it), next(it), next(it)
    c = pl.program_id(1)
    P = h_ref.shape[0]

    @pl.when(c == 0)
    def _():
        h_ref[...] = h0_ref[0]

    pairs = lambda x: jnp.stack([x[:, q * LANES:(q + 1) * LANES] for q in range(P)], axis=0)
    H = h_ref[...]
    y = pairs(y0_ref[0, 0]) + _bmm(pairs(rh_ref[0, 0]), H)
    h_ref[...] = _bmm(m_ref[0, 0, 0], H) + n_ref[0, 0, 0]
    if final:
        masks = _head_masks()
        y = y + pairs(yo_ref[0])
        mu = _head_sum(y, masks) * (1.0 / RW_N)
        dv = y - mu
        var = _head_sum(dv * dv, masks) * (1.0 / RW_N)
        gate = _mm(jax.nn.sigmoid(lg_ref[0]), g2_ref[...])
        y = (dv * lax.rsqrt(var + GN_EPS) * pairs(gg_ref[...]) + pairs(gb_ref[...])) * pairs(gate)
    for q in range(P):
        y_ref[0, :, q * LANES:(q + 1) * LANES] = y[q]

    @pl.when(c == pl.num_programs(1) - 1)
    def _():
        ht_ref[0] = h_ref[...]


def wkv_scan(parts, h0, direction, final=None):
    rh, y0, m, n = parts
    _, B, T, _ = y0.shape
    Tc = WKV_CHUNK
    nc, P = T // Tc, RW_DIM // LANES
    ch = (lambda c: nc - 1 - c) if direction == 1 else (lambda c: c)
    seq = pl.BlockSpec((1, 1, Tc, RW_DIM), lambda b, c: (direction, b, ch(c), 0))
    mat = pl.BlockSpec((1, 1, 1, P, LANES, LANES), lambda b, c: (direction, b, ch(c), 0, 0, 0))
    st = pl.BlockSpec((1, P, LANES, LANES), lambda b, c: (b, 0, 0, 0))
    out = pl.BlockSpec((1, Tc, RW_DIM), lambda b, c: (b, ch(c), 0))
    args, specs = [rh, y0, m, n, h0], [seq, seq, mat, mat, st]
    if final is not None:
        y_other, us, p = final
        vec = pl.BlockSpec((1, RW_DIM), lambda b, c: (0, 0))
        args += [y_other, us, p['g2'], p['gn_g'][None], p['gn_b'][None]]
        specs += [out, pl.BlockSpec((1, Tc, LANES), lambda b, c: (b, ch(c), RW_IN // LANES - 1)),
                  pl.BlockSpec((GATE_LORA, RW_DIM), lambda b, c: (0, 0)), vec, vec]
    return pl.pallas_call(
        functools.partial(_wkv_scan_kernel, final=final is not None), grid=(B, nc),
        in_specs=specs, out_specs=[out, st],
        out_shape=[jax.ShapeDtypeStruct((B, T, RW_DIM), f32), jax.ShapeDtypeStruct((B, P, LANES, LANES), f32)],
        scratch_shapes=[pltpu.VMEM((P, LANES, LANES), f32)],
        compiler_params=_cparams(("parallel", "arbitrary")),
        name="wkv_scan",
    )(*args)


def rwkv_mixer(u, uc, p, with_ctx_out):
    B = u.shape[0]
    us, usc = dwconv(u, p['shift_w']), dwconv(uc, p['shift_w'])
    lat, cx = wkv_chunk(us, p), wkv_chunk(usc, p)
    zero = jnp.zeros((B, RW_DIM // LANES, LANES, LANES), f32)
    yc_b, h_b = wkv_scan(cx, zero, 1)
    y_b, _ = wkv_scan(lat, h_b, 1)
    yc, h_f = wkv_scan(cx, zero, 0, final=(yc_b, usc, p) if with_ctx_out else None)
    y, _ = wkv_scan(lat, h_f, 0, final=(y_b, us, p))
    return y, (yc if with_ctx_out else None)


def _natten_kernel(q_ref, k0_ref, k1_ref, k2_ref, v0_ref, v1_ref, v2_ref, kc_ref, vc_ref,
                   b0_ref, b1_ref, qg_ref, kg_ref, o_ref):
    masks = _head_masks()
    nq = q_ref.shape[1]

    def norm(x, g):
        return x * lax.rsqrt(_head_sum(x * x, masks) * (1.0 / NA_DIM) + EPS) * g

    q = norm(q_ref[0], qg_ref[...]) * (NA_DIM ** -0.5)
    ks = [norm(r[0], kg_ref[...]).astype(bf16) for r in (k0_ref, k1_ref, k2_ref, kc_ref)]
    vs = [r[0].astype(bf16) for r in (v0_ref, v1_ref, v2_ref, vc_ref)]
    out = jnp.zeros(o_ref.shape[1:], f32)
    for h, bias_ref in enumerate((b0_ref, b1_ref)):
        qh = (q * masks[h]).astype(bf16)
        s = [_mm_nt(qh, ks[j]) + bias_ref[0, 0, :, j * nq:(j + 1) * nq] for j in range(3)]
        s.append(_mm_nt(qh, ks[3]))
        m = functools.reduce(jnp.maximum, [jnp.max(x, axis=-1, keepdims=True) for x in s])
        pr = [jnp.exp(x - m) for x in s]
        l = functools.reduce(jnp.add, [jnp.sum(x, axis=-1, keepdims=True) for x in pr])
        o = functools.reduce(jnp.add, [jnp.dot(pp.astype(bf16), vv, preferred_element_type=f32)
                                       for pp, vv in zip(pr, vs)])
        out = out + (o / l) * masks[h]
    o_ref[0] = out


def _natten_bias_table(rpb, rows):
    W = GRID_W
    kh, kw = min(WIN_H, rows), min(WIN_W, W)
    c = np.arange(W)[:, None]
    kc = np.arange(W)[None, :]
    cs = np.clip(c - kw // 2, 0, W - kw)
    col_ok = (kc >= cs) & (kc < cs + kw)
    col_hot = (col_ok[..., None] & ((kc - c + (WIN_W - 1))[..., None] == np.arange(2 * WIN_W - 1))).astype(np.float32)
    tabs = []
    for r0, bs in ((0, 0), (NA_QROWS, 0), (rows - NA_QROWS, rows - NA_BAND)):
        r = r0 + np.arange(NA_QROWS)[:, None]
        kr = bs + np.arange(NA_BAND)[None, :]
        rs = np.clip(r - kh // 2, 0, rows - kh)
        row_ok = (kr >= rs) & (kr < rs + kh)
        row_hot = (row_ok[..., None] & ((kr - r + (WIN_H - 1))[..., None] == np.arange(2 * WIN_H - 1))).astype(np.float32)
        t = jnp.einsum('rkd,cje,hde->hrckj', row_hot, col_hot, rpb, precision=HIGHEST)
        ok = row_ok[:, None, :, None] & col_ok[None, :, None, :]
        t = jnp.where(ok[None], t, MASK_VALUE)
        tabs.append(t.reshape(rpb.shape[0], NA_QROWS * W, NA_BAND * W))
    return jnp.stack(tabs, axis=1)


def natten(u, uc, q_g, k_g, rpb):
    B, T, _ = u.shape
    L = uc.shape[1]
    rows = T // GRID_W
    assert rows % NA_QROWS == 0 and rows >= NA_BAND and NA_BAND == 3 * NA_QROWS
    nb = rows // NA_QROWS
    nq = NA_QROWS * GRID_W
    P = NA_HEADS * NA_DIM // LANES
    table = _natten_bias_table(rpb.astype(f32), rows)

    def band(base, j):
        return pl.BlockSpec((1, nq, LANES), lambda b, q, i: (b, jnp.clip(i - 1, 0, nb - 3) + j, base + q))

    ctx = lambda base: pl.BlockSpec((1, L, LANES), lambda b, q, i: (b, 0, base + q))
    cls = lambda i: jnp.where(i == 0, 0, jnp.where(i == nb - 1, 2, 1))
    bias = lambda h: pl.BlockSpec((1, 1, nq, 3 * nq), lambda b, q, i: (2 * q + h, cls(i), 0, 0))
    vec = pl.BlockSpec((1, LANES), lambda b, q, i: (0, 0))
    two = lambda g: jnp.tile(g, 2)[None]
    return pl.pallas_call(
        _natten_kernel, grid=(B, P, nb),
        in_specs=[pl.BlockSpec((1, nq, LANES), lambda b, q, i: (b, i, q)),
                  band(P, 0), band(P, 1), band(P, 2), band(2 * P, 0), band(2 * P, 1), band(2 * P, 2),
                  ctx(P), ctx(2 * P), bias(0), bias(1), vec, vec],
        out_specs=pl.BlockSpec((1, nq, LANES), lambda b, q, i: (b, i, q)),
        out_shape=jax.ShapeDtypeStruct((B, T, NA_HEADS * NA_DIM), f32),
        compiler_params=_cparams(("parallel", "parallel", "arbitrary")),
        name="natten",
    )(u, u, u, u, u, u, u, uc, uc, table, table, two(q_g), two(k_g))


def _moe_ffn_kernel(h_ref, idx_ref, gate_ref, w1_ref, w3_ref, w2_ref, o_ref):
    C = idx_ref.shape[2]
    T = h_ref.shape[1]
    tok = lax.broadcasted_iota(jnp.int32, (C, T), 1)
    onehot = jnp.where(tok == idx_ref[0, 0], 1.0, 0.0).astype(bf16)
    xs = jnp.dot(onehot, h_ref[0], preferred_element_type=f32).astype(bf16)
    a1 = jnp.dot(xs, w1_ref[0], preferred_element_type=f32)
    a3 = jnp.dot(xs, w3_ref[0], preferred_element_type=f32)
    hid = (a1 * jax.nn.sigmoid(a1) * a3).astype(bf16)
    ys = jnp.dot(hid, w2_ref[0], preferred_element_type=f32) * gate_ref[0, 0]
    o_ref[0, 0] = ys.astype(o_ref.dtype)


def moe_ffn(h, idx, gate, w1, w3, w2, *, expert_major):
    B, T, D = h.shape
    E, C = idx.shape[1], idx.shape[2]
    F = w1.shape[2]
    if expert_major:
        grid, be = (E, B), (lambda e, b: (b, e))
    else:
        grid, be = (B, E), (lambda b, e: (b, e))
    bmap = lambda *g: (be(*g)[0], 0, 0)
    emap = lambda *g: (be(*g)[1], 0, 0)
    bemap = lambda *g: (*be(*g), 0, 0)
    return pl.pallas_call(
        _moe_ffn_kernel, grid=grid,
        in_specs=[pl.BlockSpec((1, T, D), bmap),
                  pl.BlockSpec((1, 1, C, 1), bemap),
                  pl.BlockSpec((1, 1, C, 1), bemap),
                  pl.BlockSpec((1, D, F), emap),
                  pl.BlockSpec((1, D, F), emap),
                  pl.BlockSpec((1, F, D), emap)],
        out_specs=pl.BlockSpec((1, 1, C, D), bemap),
        out_shape=jax.ShapeDtypeStruct((B, E, C, D), bf16),
        compiler_params=_cparams(("parallel", "arbitrary")),
        name="moe_ffn",
    )(h, idx[..., None], gate[..., None].astype(f32), w1, w3, w2)


def _moe_scatter_kernel(ys_ref, idx_ref, x_ref, g_ref, o_ref, *, chunk):
    tt = x_ref.shape[1]
    EC = ys_ref.shape[1]
    t0 = pl.program_id(2) * tt
    tok = lax.broadcasted_iota(jnp.int32, (tt, chunk), 0) + t0
    acc = jnp.zeros(o_ref.shape[1:], f32)
    for s in range(EC // chunk):
        onehot = jnp.where(tok == idx_ref[0, :, s * chunk:(s + 1) * chunk], 1.0, 0.0).astype(bf16)
        acc = acc + jnp.dot(onehot, ys_ref[0, s * chunk:(s + 1) * chunk, :], preferred_element_type=f32)
    o_ref[0] = x_ref[0] + g_ref[0] * acc


def moe_scatter(ys, idx, x, g, *, tt=512, dn=512):
    B, E, C, D = ys.shape
    T = x.shape[1]
    EC = E * C
    tt = _pick_tile(T, tt, 8)
    dn = _pick_tile(D, dn, LANES)
    chunk = _pick_tile(EC, 512, LANES)
    gmap = (lambda b, d, i: (b, 0, d)) if g.shape[0] > 1 else (lambda b, d, i: (0, 0, d))
    return pl.pallas_call(
        functools.partial(_moe_scatter_kernel, chunk=chunk), grid=(B, D // dn, T // tt),
        in_specs=[pl.BlockSpec((1, EC, dn), lambda b, d, i: (b, 0, d)),
                  pl.BlockSpec((1, 1, EC), lambda b, d, i: (b, 0, 0)),
                  pl.BlockSpec((1, tt, dn), lambda b, d, i: (b, i, d)),
                  pl.BlockSpec((1, 1, dn), gmap)],
        out_specs=pl.BlockSpec((1, tt, dn), lambda b, d, i: (b, i, d)),
        out_shape=jax.ShapeDtypeStruct((B, T, D), f32),
        compiler_params=_cparams(("parallel", "parallel", "arbitrary")),
        name="moe_scatter",
    )(ys.reshape(B, EC, D), idx.reshape(B, 1, EC), x, g.astype(f32))


def expert_choice_ffn(x, scale, shift, g, router, w1, w3, w2, *, expert_major):
    B, T, D = x.shape
    E = router.shape[1]
    cap = max(1, EC_FACTOR * T // E)
    router_p = jnp.pad(router, ((0, 0), (0, LANES - E))).astype(bf16)
    logits, h = linear(x, router_p, norm=True, scale=scale, shift=shift, emit_x=True)
    aff = jax.nn.softmax(logits[..., :E], axis=-1)
    gate, idx = lax.top_k(jnp.swapaxes(aff, 1, 2), cap)
    ys = moe_ffn(h, idx, gate, w1, w3, w2, expert_major=expert_major)
    return moe_scatter(ys, idx, x, g)


def _rms(x, g):
    return x * lax.rsqrt(jnp.mean(jnp.square(x), -1, keepdims=True) + EPS) * g


def _rope_tables(n_tok, rot_dim):
    t = jnp.arange(n_tok, dtype=jnp.int32)
    row = (t // GRID_W).astype(f32)
    col = (t % GRID_W).astype(f32)
    half = rot_dim // 2
    inv = ROPE_BASE ** (-jnp.arange(0, half, 2, dtype=f32) / half)
    ar = row[:, None] * inv
    ac = col[:, None] * inv
    ang = jnp.concatenate([ar, ar, ac, ac], -1)
    return jnp.cos(ang), jnp.sin(ang)


def _rotate_half(u):
    u1, u2 = jnp.split(u, 2, -1)
    return jnp.concatenate([-u2, u1], -1)


def _rope(x, cos, sin):
    half = x.shape[-1] // 2
    rot = jnp.concatenate([_rotate_half(x[..., :half]), _rotate_half(x[..., half:])], -1)
    return x * cos[:, None, :] + rot * sin[:, None, :]


def _heads_first(x):
    return jnp.swapaxes(x, 1, 2)


def _mla_qkv(u, p, rope_cs):
    B, T, _ = u.shape
    one = lambda g: g.reshape(1, 1, -1)
    q = linear(u[..., :Q_LORA], p['w_uq'], norm=True, scale=one(p['q_norm']))
    kv = linear(u[..., Q_LORA:Q_LORA + KV_LORA], p['w_ukv'], norm=True, scale=one(p['kv_norm']))
    kr = u[..., Q_LORA + KV_LORA:MLA_IN]
    q = q.reshape(B, T, MLA_HEADS, MLA_QK)
    kv = kv.reshape(B, T, MLA_HEADS, MLA_NOPE + MLA_V)
    k_nope, v = kv[..., :MLA_NOPE], kv[..., MLA_NOPE:]
    k = jnp.concatenate([k_nope, jnp.broadcast_to(kr[:, :, None, :], (B, T, MLA_HEADS, MLA_ROPE))], -1)
    q = _rms(q, p['q_g'])
    k = _rms(k, p['k_g'])
    if rope_cs is not None:
        cos, sin = rope_cs
        q = jnp.concatenate([q[..., :MLA_NOPE], _rope(q[..., MLA_NOPE:], cos, sin)], -1)
        k = jnp.concatenate([k[..., :MLA_NOPE], _rope(k[..., MLA_NOPE:], cos, sin)], -1)
    q = q * (MLA_QK ** -0.5)
    return tuple(_heads_first(t).astype(bf16) for t in (q, k, v))


def _merge_heads(o):
    B, H, T, Dh = o.shape
    return jnp.swapaxes(o, 1, 2).reshape(B, T, H * Dh)


def even_mixer(x, ctx, mods, p, rope_cs, with_ctx_out):
    (scale, shift, gate), (cscale, cshift, cgate) = mods
    cut = (MLA_IN_PAD, 2 * CONV_CH)
    u_mla, u_cv = linear(x, p['w_in'], norm=True, scale=scale, shift=shift, splits=cut)
    uc_mla, uc_cv = linear(ctx, p['w_in'], norm=True, scale=cscale, shift=cshift, splits=cut)
    q, k, v = _mla_qkv(u_mla, p, rope_cs)
    qc, kc, vc = _mla_qkv(uc_mla, p, None)
    o_att = _merge_heads(attention(q, jnp.concatenate([k, kc], 2), jnp.concatenate([v, vc], 2)))
    post = (p['dw_b'], p['ln_g'], p['ln_b'])
    o_conv = dwconv(u_cv, p['dw_w'], glu=True, post=post)
    half = MLA_HEADS * MLA_V
    out = lambda res, g, oa, oc: linear(oa, p['w_out'][:half], x2=oc, w2=p['w_out'][half:], res=res, gate=g)
    x_new = out(x, gate, o_att, o_conv)
    if not with_ctx_out:
        return x_new, None
    oc_att = _merge_heads(attention(qc, kc, vc))
    oc_conv = dwconv(uc_cv, p['dw_w'], glu=True, post=post)
    return x_new, out(ctx, cgate, oc_att, oc_conv)


def _natten_ctx_qkv(u, p):
    B, T, _ = u.shape
    q, k, v = [t.reshape(B, T, NA_HEADS, NA_DIM) for t in jnp.split(u, 3, -1)]
    q = _rms(q, p['na_q_g']) * (NA_DIM ** -0.5)
    k = _rms(k, p['na_k_g'])
    return tuple(_heads_first(t).astype(bf16) for t in (q, k, v))


def odd_mixer(x, ctx, mods, p, with_ctx_out):
    (scale, shift, gate), (cscale, cshift, cgate) = mods
    cut = (RW_IN, NA_IN)
    u_rw, u_na = linear(x, p['w_in'], norm=True, scale=scale, shift=shift, splits=cut)
    uc_rw, uc_na = linear(ctx, p['w_in'], norm=True, scale=cscale, shift=cshift, splits=cut)
    y_rw, yc_rw = rwkv_mixer(u_rw, uc_rw, p, with_ctx_out)
    y_na = natten(u_na, uc_na, p['na_q_g'], p['na_k_g'], p['rpb'])
    out = lambda res, g, a, b: linear(a, p['w_out'][:RW_DIM], x2=b, w2=p['w_out'][RW_DIM:], res=res, gate=g)
    x_new = out(x, gate, y_rw, y_na)
    if not with_ctx_out:
        return x_new, None
    yc_na = _merge_heads(attention(*_natten_ctx_qkv(uc_na, p)))
    return x_new, out(ctx, cgate, yc_rw, yc_na)


def _lora_by_direction(w):
    z = jnp.zeros_like(w[0])
    return jnp.stack([jnp.concatenate([w[0], z], 0), jnp.concatenate([z, w[1]], 0)]).astype(bf16)


def kernel(x, c, ctx, c_ctx, ada_w, ada_b, norm1_g, norm2_g, ev_w_in, ev_w_out, mla_q_norm, mla_w_uq, mla_kv_norm, mla_w_ukv, mla_q_g, mla_k_g, cv_dw_w, cv_dw_b, cv_ln_g, cv_ln_b, od_w_in, od_w_out, rw_shift_w, rw_w0, rw_w2, rw_a0, rw_a2, rw_g2, rw_k_k, rw_k_a, rw_r_k, rw_gn_g, rw_gn_b, na_q_g, na_k_g, na_rpb, moe_router, moe_w1, moe_w3, moe_w2):
    B, T, D = x.shape
    depth = ada_w.shape[0]
    rope_cs = _rope_tables(T, MLA_ROPE)
    cond = jax.nn.silu(jnp.concatenate([c, c_ctx[None]], 0))
    cond = jnp.pad(cond, ((0, (-(B + 1)) % 8), (0, 0)))[None]
    for i in range(depth):
        last = i == depth - 1
        j = i // 2
        mod = linear(cond, ada_w[i].astype(bf16), bias=ada_b[i][None])[0]
        sh1, sc1, g1, sh2, sc2, g2 = (t[:, None, :] for t in jnp.split(mod[:B], 6, -1))
        csh1, csc1, cg1, csh2, csc2, cg2 = (t[None] for t in jnp.split(mod[B:B + 1], 6, -1))
        n1, n2 = norm1_g[i], norm2_g[i]
        mods = ((n1 * (1.0 + sc1), sh1, g1), (n1 * (1.0 + csc1), csh1, cg1))
        if i % 2 == 0:
            w_in = ev_w_in[j]
            w_in = jnp.concatenate([jnp.pad(w_in[:, :MLA_IN], ((0, 0), (0, MLA_IN_PAD - MLA_IN))), w_in[:, MLA_IN:]], 1)
            p = dict(w_in=w_in.astype(bf16), w_out=ev_w_out[j].astype(bf16), q_norm=mla_q_norm[j],
                     w_uq=mla_w_uq[j].astype(bf16), kv_norm=mla_kv_norm[j], w_ukv=mla_w_ukv[j].astype(bf16),
                     q_g=mla_q_g[j], k_g=mla_k_g[j], dw_w=cv_dw_w[j], dw_b=cv_dw_b[j], ln_g=cv_ln_g[j], ln_b=cv_ln_b[j])
            x, ctx_mix = even_mixer(x, ctx, mods, p, rope_cs, not last)
        else:
            p = dict(w_in=od_w_in[j].astype(bf16), w_out=od_w_out[j].astype(bf16), shift_w=rw_shift_w[j],
                     w0=rw_w0[j], w2cat=_lora_by_direction(rw_w2[j]), a0=rw_a0[j], a2cat=_lora_by_direction(rw_a2[j]),
                     g2=rw_g2[j].astype(bf16), k_k=rw_k_k[j], k_a=rw_k_a[j], r_k=rw_r_k[j],
                     gn_g=rw_gn_g[j], gn_b=rw_gn_b[j], na_q_g=na_q_g[j], na_k_g=na_k_g[j], rpb=na_rpb[j])
            x, ctx_mix = odd_mixer(x, ctx, mods, p, not last)
        w1, w3, w2 = (w[i].astype(bf16) for w in (moe_w1, moe_w3, moe_w2))
        x = expert_choice_ffn(x, n2 * (1.0 + sc2), sh2, g2, moe_router[i], w1, w3, w2, expert_major=False)
        if not last:
            ctx = expert_choice_ffn(ctx_mix, n2 * (1.0 + csc2), csh2, cg2, moe_router[i], w1, w3, w2, expert_major=True)
    return x
```

```python
import functools
import math

import numpy as np
import jax
import jax.numpy as jnp
from jax import lax
from jax.experimental import pallas as pl
from jax.experimental.pallas import tpu as pltpu

f32 = jnp.float32
bf16 = jnp.bfloat16
HIGHEST = lax.Precision.HIGHEST

D_MODEL = 1024
GRID_W = 64
ROPE_BASE = 10000.0
EPS = 1e-6
LANES = 128

MLA_HEADS = D_MODEL // 128
MLA_NOPE = 64
MLA_ROPE = 32
MLA_QK = MLA_NOPE + MLA_ROPE
MLA_V = 64
Q_LORA = 3 * D_MODEL // 8
KV_LORA = D_MODEL // 4
MLA_IN = Q_LORA + KV_LORA + MLA_ROPE
MLA_IN_PAD = 768
CONV_CH = D_MODEL // 2
CONV_K = 31

RW_N = 64
RW_HEADS = D_MODEL // 128
RW_DIM = RW_HEADS * RW_N
DECAY_LORA = 64
ICLR_LORA = 64
GATE_LORA = 128
SHIFT_K = 3
RW_IN = 3 * RW_DIM + 2 * DECAY_LORA + 2 * ICLR_LORA + GATE_LORA
GN_EPS = 64e-5

NA_HEADS = D_MODEL // 128
NA_DIM = 64
WIN_H = 8
WIN_W = 16
NA_IN = 3 * NA_HEADS * NA_DIM
NA_QROWS = 4
NA_BAND = 12

N_EXPERTS = 16
EC_FACTOR = 2

WKV_CHUNK = 64
WKV_INV_BASE = 16
CONV_HALO = 32
MASK_VALUE = -1e30

VMEM_LIMIT = 56 * 1024 * 1024


def _cparams(sem):
    return pltpu.CompilerParams(dimension_semantics=sem, vmem_limit_bytes=VMEM_LIMIT)


def _mm(a, b):
    return jnp.dot(a.astype(bf16), b.astype(bf16), preferred_element_type=f32)


def _mm_nt(a, b):
    return lax.dot_general(a.astype(bf16), b.astype(bf16), (((1,), (1,)), ((), ())), preferred_element_type=f32)


def _mm_tn(a, b):
    return lax.dot_general(a.astype(bf16), b.astype(bf16), (((0,), (0,)), ((), ())), preferred_element_type=f32)


def _head_masks():
    lane = lax.broadcasted_iota(jnp.int32, (1, LANES), 1)
    m0 = jnp.where(lane < LANES // 2, 1.0, 0.0).astype(f32)
    return m0, 1.0 - m0


def _head_sum(x, masks):
    s0 = jnp.sum(x * masks[0], axis=-1, keepdims=True)
    s1 = jnp.sum(x * masks[1], axis=-1, keepdims=True)
    return s0 * masks[0] + s1 * masks[1]


def _linear_kernel(*refs, norm, has_scale, has_shift, has_x2, has_bias, has_res, emit_x, splits):
    it = iter(refs)
    x_ref, w_ref = next(it), next(it)
    scale_ref = next(it) if has_scale else None
    shift_ref = next(it) if has_shift else None
    x2_ref = next(it) if has_x2 else None
    w2_ref = next(it) if has_x2 else None
    bias_ref = next(it) if has_bias else None
    res_ref = next(it) if has_res else None
    gate_ref = next(it) if has_res else None
    o_refs = [next(it) for _ in splits]
    xo_ref = next(it) if emit_x else None
    xb_ref = next(it)

    @pl.when(pl.program_id(2) == 0)
    def _():
        x = x_ref[0].astype(f32)
        if norm:
            x = x * lax.rsqrt(jnp.mean(x * x, axis=-1, keepdims=True) + EPS)
        if has_scale:
            x = x * scale_ref[0]
        if has_shift:
            x = x + shift_ref[0]
        xb_ref[...] = x.astype(bf16)
        if emit_x:
            xo_ref[0] = xb_ref[...]

    lo = 0
    for o_ref, width in zip(o_refs, splits):
        cols = slice(lo, lo + width) if len(splits) > 1 else slice(None)
        acc = jnp.dot(xb_ref[...], w_ref[:, cols], preferred_element_type=f32)
        if has_x2:
            acc = acc + jnp.dot(x2_ref[0].astype(bf16), w2_ref[:, cols], preferred_element_type=f32)
        if has_bias:
            acc = acc + bias_ref[:, cols]
        if has_res:
            acc = res_ref[0] + gate_ref[0] * acc
        o_ref[0] = acc.astype(o_ref.dtype)
        lo += width


def _pick_tile(n, target, align):
    if n <= target:
        return n
    t = (target // align) * align
    while t > align and n % t:
        t -= align
    assert n % t == 0, (n, target, align)
    return t


def linear(x, w, *, norm=False, scale=None, shift=None, x2=None, w2=None, bias=None,
           res=None, gate=None, emit_x=False, splits=None, tm=512, tn=None, out_dtype=f32):
    B, T, K = x.shape
    N = w.shape[1]
    assert w.shape[0] == K and N % LANES == 0, (x.shape, w.shape)
    tm = _pick_tile(T, tm, 8)
    if splits is None:
        tn = _pick_tile(N, 2048 if tn is None else tn, LANES)
        widths = (tn,)
    else:
        assert sum(splits) == N and all(s % LANES == 0 for s in splits) and res is None
        tn, widths = N, tuple(splits)
    grid = (B, T // tm, N // tn)

    def bvec(a):
        return (lambda b, i, j: (b, 0, 0)) if a.shape[0] > 1 else (lambda b, i, j: (0, 0, 0))

    args = [x, w]
    specs = [pl.BlockSpec((1, tm, K), lambda b, i, j: (b, i, 0)),
             pl.BlockSpec((K, tn), lambda b, i, j: (0, j))]
    if scale is not None:
        args.append(scale.astype(f32))
        specs.append(pl.BlockSpec((1, 1, K), bvec(scale)))
    if shift is not None:
        args.append(shift.astype(f32))
        specs.append(pl.BlockSpec((1, 1, K), bvec(shift)))
    if x2 is not None:
        K2 = x2.shape[-1]
        args += [x2, w2]
        specs += [pl.BlockSpec((1, tm, K2), lambda b, i, j: (b, i, 0)),
                  pl.BlockSpec((K2, tn), lambda b, i, j: (0, j))]
    if bias is not None:
        args.append(bias.astype(f32))
        specs.append(pl.BlockSpec((1, tn), lambda b, i, j: (0, j)))
    if res is not None:
        args += [res, gate.astype(f32)]
        specs += [pl.BlockSpec((1, tm, tn), lambda b, i, j: (b, i, j)),
                  pl.BlockSpec((1, 1, tn), (lambda b, i, j: (b, 0, j)) if gate.shape[0] > 1
                               else (lambda b, i, j: (0, 0, j)))]
    if splits is None:
        out_shape = [jax.ShapeDtypeStruct((B, T, N), out_dtype)]
        out_specs = [pl.BlockSpec((1, tm, tn), lambda b, i, j: (b, i, j))]
    else:
        out_shape = [jax.ShapeDtypeStruct((B, T, s), out_dtype) for s in splits]
        out_specs = [pl.BlockSpec((1, tm, s), lambda b, i, j: (b, i, 0)) for s in splits]
    if emit_x:
        out_shape.append(jax.ShapeDtypeStruct((B, T, K), bf16))
        out_specs.append(pl.BlockSpec((1, tm, K), lambda b, i, j: (b, i, 0)))
    kern = functools.partial(_linear_kernel, norm=norm, has_scale=scale is not None,
                             has_shift=shift is not None, has_x2=x2 is not None,
                             has_bias=bias is not None, has_res=res is not None, emit_x=emit_x, splits=widths)
    outs = pl.pallas_call(
        kern, grid=grid, in_specs=specs, out_specs=out_specs, out_shape=out_shape,
        scratch_shapes=[pltpu.VMEM((tm, K), bf16)],
        compiler_params=_cparams(("parallel", "parallel", "arbitrary")),
        name="linear",
    )(*args)
    return outs if (emit_x or splits is not None) else outs[0]


ATT_DV = LANES // 2


def _attn_kernel(*refs):
    q_ref, o_ref = refs[0], refs[-1]
    kv = refs[1:-1]
    lane = lax.broadcasted_iota(jnp.int32, (1, LANES), 1)
    res = []
    for h in range(2):
        q = q_ref[0, h]
        s = [_mm_nt(q, k_ref[0, h]) for k_ref in kv[0::2]]
        m = functools.reduce(jnp.maximum, [jnp.max(x, axis=-1, keepdims=True) for x in s])
        o = functools.reduce(jnp.add, [jnp.dot(jnp.exp((x - m).astype(bf16)), v_ref[0, h], preferred_element_type=f32)
                                       for x, v_ref in zip(s, kv[1::2])])
        res.append(o / o[:, ATT_DV:ATT_DV + 1])
    o_ref[0] = jnp.where(lane < ATT_DV, res[0], pltpu.roll(res[1], ATT_DV, axis=1))


def attention(q, kvs, *, tq=256):
    B, H, T, _ = q.shape
    assert H % 2 == 0
    tq = _pick_tile(T, tq, 8)
    args, specs = [q], [pl.BlockSpec((1, 2, tq, LANES), lambda b, h, i: (b, h, i, 0))]
    for k, v in kvs:
        args += [k, v]
        specs += [pl.BlockSpec((1, 2, k.shape[2], LANES), lambda b, h, i: (b, h, 0, 0))] * 2
    return pl.pallas_call(
        _attn_kernel, grid=(B, H // 2, T // tq), in_specs=specs,
        out_specs=pl.BlockSpec((1, tq, LANES), lambda b, h, i: (b, i, h)),
        out_shape=jax.ShapeDtypeStruct((B, T, H * ATT_DV), f32),
        compiler_params=_cparams(("parallel", "parallel", "arbitrary")),
        name="attention",
    )(*args)


def _mla_prep_kernel(u_ref, qn_ref, kvn_ref, wq_ref, wkv_ref, cq_ref, sq_ref, ck_ref, sk_ref,
                     q_ref, k_ref, v_ref):
    u = u_ref[0]
    H = q_ref.shape[1]

    def low_rank_norm(x, g_ref):
        return (x * lax.rsqrt(jnp.mean(x * x, axis=-1, keepdims=True) + EPS) * g_ref[...]).astype(bf16)

    qa = jnp.dot(low_rank_norm(u[:, :Q_LORA], qn_ref), wq_ref[...], preferred_element_type=f32)
    kva = jnp.dot(low_rank_norm(u[:, Q_LORA:Q_LORA + KV_LORA], kvn_ref), wkv_ref[...], preferred_element_type=f32)
    shared = pltpu.roll(u[:, Q_LORA + KV_LORA:], MLA_NOPE, axis=1)
    lane = lax.broadcasted_iota(jnp.int32, (1, LANES), 1)
    real = lane < MLA_QK
    ones_lane = jnp.where(lane == ATT_DV, 1.0, 0.0)

    def head_norm_rope(x, cos_ref, sin_ref, scale):
        ms = jnp.sum(jnp.where(real, x * x, 0.0), axis=-1, keepdims=True) * (1.0 / MLA_QK)
        y = x * cos_ref[...] + pltpu.roll(x, LANES - MLA_ROPE, axis=1) * sin_ref[...]
        return (y * (lax.rsqrt(ms + EPS) * scale)).astype(bf16)

    for h in range(H):
        cols = slice(h * LANES, (h + 1) * LANES)
        q_ref[0, h] = head_norm_rope(qa[:, cols], cq_ref, sq_ref, MLA_QK ** -0.5)
        k_ref[0, h] = head_norm_rope(kva[:, cols] + shared, ck_ref, sk_ref, 1.0)
        v_ref[0, h] = (kva[:, H * LANES + h * LANES:H * LANES + (h + 1) * LANES] + ones_lane).astype(bf16)


_ROPE_PERM = np.array([8, 9, 10, 11, 12, 13, 14, 15, 0, 1, 2, 3, 4, 5, 6, 7,
                       24, 25, 26, 27, 28, 29, 30, 31, 16, 17, 18, 19, 20, 21, 22, 23])
_ROPE_SIGN = np.array([-1.0] * 8 + [1.0] * 8 + [-1.0] * 8 + [1.0] * 8, np.float32)


def _mla_tables(g, rope_cs, n_tok):
    if rope_cs is None:
        cos, sin = jnp.ones((n_tok, MLA_ROPE), f32), jnp.zeros((n_tok, MLA_ROPE), f32)
    else:
        cos, sin = rope_cs
    pad = jnp.zeros((n_tok, LANES - MLA_QK), f32)
    c = jnp.concatenate([jnp.broadcast_to(g[:MLA_NOPE], (n_tok, MLA_NOPE)), g[MLA_NOPE:] * cos, pad], -1)
    s = jnp.concatenate([jnp.zeros((n_tok, MLA_NOPE), f32), g[MLA_NOPE:][_ROPE_PERM] * sin, pad], -1)
    return c, s


def mla_prep(u, p, rope_cs, *, tm=256):
    B, T, W = u.shape
    H = MLA_HEADS
    tm = _pick_tile(T, tm, 8)
    cq, sq = _mla_tables(p['q_g'], rope_cs, T)
    ck, sk = _mla_tables(p['k_g'], rope_cs, T)
    row = lambda a: pl.BlockSpec((1, a.shape[1]), lambda b, i: (0, 0))
    tab = pl.BlockSpec((tm, LANES), lambda b, i: (i, 0))
    full = lambda a: pl.BlockSpec(a.shape, lambda b, i: (0, 0))
    out = pl.BlockSpec((1, H, tm, LANES), lambda b, i: (b, 0, i, 0))
    shape = jax.ShapeDtypeStruct((B, H, T, LANES), bf16)
    qn, kvn = p['q_norm'][None], p['kv_norm'][None]
    return pl.pallas_call(
        _mla_prep_kernel, grid=(B, T // tm),
        in_specs=[pl.BlockSpec((1, tm, W), lambda b, i: (b, i, 0)), row(qn), row(kvn),
                  full(p['wq_blocks']), full(p['wkv_blocks']), tab, tab, tab, tab],
        out_specs=[out, out, out], out_shape=[shape, shape, shape],
        compiler_params=_cparams(("parallel", "parallel")),
        name="mla_prep",
    )(u, qn, kvn, p['wq_blocks'], p['wkv_blocks'], cq, sq, ck, sk)


def _mla_weight_blocks(w_in, w_uq, w_ukv):
    H = MLA_HEADS
    partner = lambda w: w[..., MLA_NOPE + _ROPE_PERM] * _ROPE_SIGN
    d = w_in.shape[0]
    kr = w_in[:, Q_LORA + KV_LORA:MLA_IN]
    w_in_mla = jnp.concatenate([w_in[:, :MLA_IN], kr[:, _ROPE_PERM] * _ROPE_SIGN,
                                jnp.zeros((d, MLA_IN_PAD - MLA_IN - MLA_ROPE), w_in.dtype)], 1)
    wq = w_uq.reshape(Q_LORA, H, MLA_QK)
    wq = jnp.concatenate([wq, partner(wq)], -1).reshape(Q_LORA, H * LANES)
    wkv = w_ukv.reshape(KV_LORA, H, MLA_NOPE + MLA_V)
    z = jnp.zeros((KV_LORA, H, LANES - MLA_NOPE), w_ukv.dtype)
    wk = jnp.concatenate([wkv[..., :MLA_NOPE], z], -1).reshape(KV_LORA, H * LANES)
    wv = jnp.concatenate([wkv[..., MLA_NOPE:], z], -1).reshape(KV_LORA, H * LANES)
    return w_in_mla, wq.astype(bf16), jnp.concatenate([wk, wv], 1).astype(bf16)


def _dwconv_kernel(*refs, taps, tb, C, glu, post):
    it = iter(refs)
    xp_ref, xm_ref, xn_ref, w_ref = next(it), next(it), next(it), next(it)
    if post:
        b_ref, g_ref, bb_ref = next(it), next(it), next(it)
    o_ref = next(it)
    hs_ref = next(it)
    i = pl.program_id(1)

    def pre(x):
        if glu:
            return x[:, :C] * jax.nn.sigmoid(x[:, C:])
        return x

    hs_ref[0:CONV_HALO, :] = pre(xp_ref[0]) * jnp.where(i > 0, 1.0, 0.0)
    hs_ref[CONV_HALO:CONV_HALO + tb, :] = pre(xm_ref[0])
    hs_ref[CONV_HALO + tb:, :] = pre(xn_ref[0]) * jnp.where(i < pl.num_programs(1) - 1, 1.0, 0.0)
    off = CONV_HALO - (taps - 1) // 2
    acc = jnp.zeros((tb, C), f32)
    for j in range(taps):
        acc = acc + w_ref[j:j + 1, :] * hs_ref[off + j:off + j + tb, :]
    if post:
        acc = acc + b_ref[...]
        mu = jnp.mean(acc, axis=-1, keepdims=True)
        d = acc - mu
        var = jnp.mean(d * d, axis=-1, keepdims=True)
        y = d * lax.rsqrt(var + EPS) * g_ref[...] + bb_ref[...]
        acc = y * jax.nn.sigmoid(y)
    o_ref[0] = acc


def dwconv(x, w, *, glu=False, post=None, tb=256):
    B, T, Cin = x.shape
    taps, C = w.shape
    assert Cin == (2 * C if glu else C) and (taps - 1) // 2 <= CONV_HALO
    tb = _pick_tile(T, tb, CONV_HALO)
    assert tb % CONV_HALO == 0
    r, nh = tb // CONV_HALO, T // CONV_HALO
    args = [x, x, x, w]
    specs = [pl.BlockSpec((1, CONV_HALO, Cin), lambda b, i: (b, jnp.maximum(i * r - 1, 0), 0)),
             pl.BlockSpec((1, tb, Cin), lambda b, i: (b, i, 0)),
             pl.BlockSpec((1, CONV_HALO, Cin), lambda b, i: (b, jnp.minimum((i + 1) * r, nh - 1), 0)),
             pl.BlockSpec((taps, C), lambda b, i: (0, 0))]
    if post is not None:
        args += [p.reshape(1, C) for p in post]
        specs += [pl.BlockSpec((1, C), lambda b, i: (0, 0))] * 3
    kern = functools.partial(_dwconv_kernel, taps=taps, tb=tb, C=C, glu=glu, post=post is not None)
    return pl.pallas_call(
        kern, grid=(B, T // tb), in_specs=specs,
        out_specs=pl.BlockSpec((1, tb, C), lambda b, i: (b, i, 0)),
        out_shape=jax.ShapeDtypeStruct((B, T, C), f32),
        scratch_shapes=[pltpu.VMEM((tb + 2 * CONV_HALO, C), f32)],
        compiler_params=_cparams(("parallel", "parallel")),
        name="dwconv",
    )(*args)


def _bmm(a, b):
    return jnp.einsum('gij,gjk->gik', a.astype(bf16), b.astype(bf16), preferred_element_type=f32)


def _bmm_nt(a, b):
    return jnp.einsum('gik,gjk->gij', a.astype(bf16), b.astype(bf16), preferred_element_type=f32)


def _bmm_tn(a, b):
    return lax.dot_general(a.astype(bf16), b.astype(bf16), (((1,), (1,)), ((0,), (0,))),
                           preferred_element_type=f32)


def _wkv_units(r, k_eff, v, lw, a, b, sgn, masks):
    G, Tc, _ = r.shape
    ti = lax.broadcasted_iota(jnp.int32, (G, Tc, Tc), 1)
    tj = lax.broadcasted_iota(jnp.int32, (G, Tc, Tc), 2)
    tri = jnp.where((tj - ti) * sgn <= 0, 1.0, 0.0).astype(bf16)
    hi = lw.astype(bf16)
    rest = lw - hi.astype(f32)
    mid = rest.astype(bf16)
    lo = (rest - mid.astype(f32)).astype(bf16)
    L = _bmm(tri, hi) + _bmm(tri, mid) + _bmm(tri, lo)
    Ltot = jnp.sum(lw, axis=1, keepdims=True)
    enL = jnp.exp(-L)
    eR = jnp.exp(Ltot - L)
    At, Rt, Bt, Kt, Bb, Kb = a * jnp.exp(L - lw), r * jnp.exp(L), b * enL, k_eff * enL, b * eR, k_eff * eR

    stack = lambda x: jnp.concatenate([x * masks[0], x * masks[1]], axis=1)
    Y, X, Xk, Vs = stack(At), stack(Bt), stack(Kt), stack(v)
    n2 = 2 * Tc
    gi = lax.broadcasted_iota(jnp.int32, (G, n2, n2), 1)
    gj = lax.broadcasted_iota(jnp.int32, (G, n2, n2), 2)
    before = ((gj & (Tc - 1)) - (gi & (Tc - 1))) * sgn < 0
    Aab = jnp.where(before, _bmm_nt(Y, X), 0.0)
    Aak = jnp.where(before, _bmm_nt(Y, Xk), 0.0)
    ri = lax.broadcasted_iota(jnp.int32, (G, Tc, 2 * n2), 1)
    rj = lax.broadcasted_iota(jnp.int32, (G, Tc, 2 * n2), 2)
    upto = ((rj & (Tc - 1)) - ri) * sgn <= 0
    RB = jnp.where(upto, _bmm_nt(Rt, jnp.concatenate([X, Xk], axis=1)), 0.0)

    same = lambda size: jnp.right_shift(gi, int(math.log2(size))) == jnp.right_shift(gj, int(math.log2(size)))
    Xp = jnp.where(same(WKV_INV_BASE), Aab, 0.0)
    Tm = jnp.where(gi == gj, 1.0, 0.0) + Xp
    span = 2
    while span < WKV_INV_BASE:
        Xp = _bmm(Xp, Xp)
        Tm = Tm + _bmm(Tm, Xp)
        span *= 2
    size = WKV_INV_BASE
    while size < Tc:
        off = jnp.where(same(2 * size) & jnp.logical_not(same(size)), Aab, 0.0)
        Tm = Tm + _bmm(_bmm(Tm, off), Tm)
        size *= 2

    TA = _bmm(Tm, jnp.concatenate([_bmm(Aak, Vs), Y], axis=2))
    U0, Ah = TA[:, :, :LANES], TA[:, :, LANES:]
    UV = jnp.concatenate([U0, Vs], axis=1)
    Bs = stack(Bb)
    ki = lax.broadcasted_iota(jnp.int32, (G, LANES, LANES), 1)
    kj = lax.broadcasted_iota(jnp.int32, (G, LANES, LANES), 2)
    M = _bmm_tn(Bs, Ah) + jnp.where(ki == kj, jnp.exp(Ltot), 0.0)
    N = _bmm_tn(jnp.concatenate([Bs, stack(Kb)], axis=1), UV)
    return Rt + _bmm(RB[:, :, :n2], Ah), _bmm(RB, UV), M, N


def _wkv_chunk_kernel(r_ref, k_ref, v_ref, lw_ref, la_ref, w2_ref, w0_ref, a2_ref, a0_ref,
                      kk_ref, ka_ref, rk_ref, rh_ref, y0_ref, m_ref, n_ref):
    masks = _head_masks()
    P = r_ref.shape[2] // LANES
    pairs = lambda x: jnp.stack([x[:, q * LANES:(q + 1) * LANES] for q in range(P)], axis=0)
    lw_in, la_in = jnp.tanh(lw_ref[0]), la_ref[0]
    r, k, v = pairs(r_ref[0]), pairs(k_ref[0]), pairs(v_ref[0])
    kk = k * pairs(kk_ref[...])
    kk = kk * lax.rsqrt(_head_sum(kk * kk, masks) + 1e-12)
    lw, k_eff, b, bonus = [], [], [], []
    for d in range(2):
        logw = w0_ref[d] + _mm(lw_in, w2_ref[d])
        lw.append(pairs(-jnp.exp(-jax.nn.softplus(-logw) - 0.5)))
        rate = pairs(jax.nn.sigmoid(a0_ref[d] + _mm(la_in, a2_ref[d])))
        k_eff.append(k * (1.0 + (rate - 1.0) * pairs(ka_ref[...])))
        b.append(kk * rate)
        bonus.append(_head_sum(r * k_eff[d] * pairs(rk_ref[...]), masks) * v)
    both = lambda x: jnp.concatenate([x, x], axis=0)
    cat = lambda xs: jnp.concatenate(xs, axis=0)
    unit = lax.broadcasted_iota(jnp.int32, (2 * P, 1, 1), 0)
    sgn = jnp.where(unit < P, 1, -1)
    rh, y0, M, N = _wkv_units(both(r), cat(k_eff), both(v), cat(lw), both(-kk), cat(b), sgn, masks)
    y0 = y0 + cat(bonus)
    for d in range(2):
        for q in range(P):
            cols = slice(q * LANES, (q + 1) * LANES)
            rh_ref[d, 0, :, cols] = rh[d * P + q].astype(rh_ref.dtype)
            y0_ref[d, 0, :, cols] = y0[d * P + q]
            m_ref[d, 0, 0, q] = M[d * P + q].astype(m_ref.dtype)
            n_ref[d, 0, 0, q] = N[d * P + q]


def wkv_chunk(us, p):
    B, T, _ = us.shape
    Tc = WKV_CHUNK
    assert T % Tc == 0 and RW_IN % RW_DIM == 3 * LANES
    nc, P = T // Tc, RW_DIM // LANES
    wide = lambda i: pl.BlockSpec((1, Tc, RW_DIM), lambda b, c: (b, c, i))
    lora_in = lambda i: pl.BlockSpec((1, Tc, LANES), lambda b, c: (b, c, 3 * P + i))
    full = lambda a: pl.BlockSpec(a.shape, lambda b, c: (0,) * a.ndim)
    seq = pl.BlockSpec((2, 1, Tc, RW_DIM), lambda b, c: (0, b, c, 0))
    mat = pl.BlockSpec((2, 1, 1, P, LANES, LANES), lambda b, c: (0, b, c, 0, 0, 0))
    params = [p['w2cat'], p['w0'][:, None, :], p['a2cat'], p['a0'][:, None, :],
              p['k_k'][None], p['k_a'][None], p['r_k'][None]]
    return pl.pallas_call(
        _wkv_chunk_kernel, grid=(B, nc),
        in_specs=[wide(0), wide(1), wide(2), lora_in(0), lora_in(1)] + [full(a) for a in params],
        out_specs=[seq, seq, mat, mat],
        out_shape=[jax.ShapeDtypeStruct((2, B, T, RW_DIM), bf16), jax.ShapeDtypeStruct((2, B, T, RW_DIM), f32),
                   jax.ShapeDtypeStruct((2, B, nc, P, LANES, LANES), bf16),
                   jax.ShapeDtypeStruct((2, B, nc, P, LANES, LANES), f32)],
        compiler_params=_cparams(("parallel", "parallel")),
        name="wkv_chunk",
    )(us, us, us, us, us, *params)


def _wkv_scan_kernel(*refs, final):
    it = iter(refs)
    rh_ref, y0_ref, m_ref, n_ref, h0_ref = (next(it) for _ in range(5))
    if final:
        yo_ref, lg_ref, g2_ref, gg_ref, gb_ref = (next(it) for _ in range(5))
    y_ref, ht_ref, h_ref = next(it), next(it), next(it)
    c = pl.program_id(1)
    P = h_ref.shape[0]

    @pl.when(c == 0)
    def _():
        h_ref[...] = h0_ref[0]

    pairs = lambda x: jnp.stack([x[:, q * LANES:(q + 1) * LANES] for q in range(P)], axis=0)
    H = h_ref[...]
    y = pairs(y0_ref[0, 0]) + _bmm(pairs(rh_ref[0, 0]), H)
    h_ref[...] = _bmm(m_ref[0, 0, 0], H) + n_ref[0, 0, 0]
    if final:
        masks = _head_masks()
        y = y + pairs(yo_ref[0])
        mu = _head_sum(y, masks) * (1.0 / RW_N)
        dv = y - mu
        var = _head_sum(dv * dv, masks) * (1.0 / RW_N)
        gate = _mm(jax.nn.sigmoid(lg_ref[0]), g2_ref[...])
        y = (dv * lax.rsqrt(var + GN_EPS) * pairs(gg_ref[...]) + pairs(gb_ref[...])) * pairs(gate)
    for q in range(P):
        y_ref[0, :, q * LANES:(q + 1) * LANES] = y[q]

    @pl.when(c == pl.num_programs(1) - 1)
    def _():
        ht_ref[0] = h_ref[...]


def wkv_scan(parts, h0, direction, final=None):
    rh, y0, m, n = parts
    _, B, T, _ = y0.shape
    Tc = WKV_CHUNK
    nc, P = T // Tc, RW_DIM // LANES
    ch = (lambda c: nc - 1 - c) if direction == 1 else (lambda c: c)
    seq = pl.BlockSpec((1, 1, Tc, RW_DIM), lambda b, c: (direction, b, ch(c), 0))
    mat = pl.BlockSpec((1, 1, 1, P, LANES, LANES), lambda b, c: (direction, b, ch(c), 0, 0, 0))
    st = pl.BlockSpec((1, P, LANES, LANES), lambda b, c: (b, 0, 0, 0))
    out = pl.BlockSpec((1, Tc, RW_DIM), lambda b, c: (b, ch(c), 0))
    args, specs = [rh, y0, m, n, h0], [seq, seq, mat, mat, st]
    if final is not None:
        y_other, us, p = final
        vec = pl.BlockSpec((1, RW_DIM), lambda b, c: (0, 0))
        args += [y_other, us, p['g2'], p['gn_g'][None], p['gn_b'][None]]
        specs += [out, pl.BlockSpec((1, Tc, LANES), lambda b, c: (b, ch(c), RW_IN // LANES - 1)),
                  pl.BlockSpec((GATE_LORA, RW_DIM), lambda b, c: (0, 0)), vec, vec]
    return pl.pallas_call(
        functools.partial(_wkv_scan_kernel, final=final is not None), grid=(B, nc),
        in_specs=specs, out_specs=[out, st],
        out_shape=[jax.ShapeDtypeStruct((B, T, RW_DIM), f32), jax.ShapeDtypeStruct((B, P, LANES, LANES), f32)],
        scratch_shapes=[pltpu.VMEM((P, LANES, LANES), f32)],
        compiler_params=_cparams(("parallel", "arbitrary")),
        name="wkv_scan",
    )(*args)


def rwkv_mixer(u, uc, p, with_ctx_out):
    B = u.shape[0]
    us, usc = dwconv(u, p['shift_w']), dwconv(uc, p['shift_w'])
    lat, cx = wkv_chunk(us, p), wkv_chunk(usc, p)
    zero = jnp.zeros((B, RW_DIM // LANES, LANES, LANES), f32)
    yc_b, h_b = wkv_scan(cx, zero, 1)
    y_b, _ = wkv_scan(lat, h_b, 1)
    yc, h_f = wkv_scan(cx, zero, 0, final=(yc_b, usc, p) if with_ctx_out else None)
    y, _ = wkv_scan(lat, h_f, 0, final=(y_b, us, p))
    return y, (yc if with_ctx_out else None)


def _natten_kernel(q_ref, k0_ref, k1_ref, k2_ref, v0_ref, v1_ref, v2_ref, kc_ref, vc_ref,
                   b0_ref, b1_ref, qg_ref, kg_ref, o_ref):
    masks = _head_masks()
    nq = q_ref.shape[1]

    def norm(x, g):
        return x * lax.rsqrt(_head_sum(x * x, masks) * (1.0 / NA_DIM) + EPS) * g

    q = norm(q_ref[0], qg_ref[...]) * (NA_DIM ** -0.5)
    ks = [norm(r[0], kg_ref[...]).astype(bf16) for r in (k0_ref, k1_ref, k2_ref, kc_ref)]
    vs = [r[0].astype(bf16) for r in (v0_ref, v1_ref, v2_ref, vc_ref)]
    out = jnp.zeros(o_ref.shape[1:], f32)
    for h, bias_ref in enumerate((b0_ref, b1_ref)):
        qh = (q * masks[h]).astype(bf16)
        s = [_mm_nt(qh, ks[j]) + bias_ref[0, 0, :, j * nq:(j + 1) * nq] for j in range(3)]
        s.append(_mm_nt(qh, ks[3]))
        m = functools.reduce(jnp.maximum, [jnp.max(x, axis=-1, keepdims=True) for x in s])
        pr = [jnp.exp(x - m) for x in s]
        l = functools.reduce(jnp.add, [jnp.sum(x, axis=-1, keepdims=True) for x in pr])
        o = functools.reduce(jnp.add, [jnp.dot(pp.astype(bf16), vv, preferred_element_type=f32)
                                       for pp, vv in zip(pr, vs)])
        out = out + (o / l) * masks[h]
    o_ref[0] = out


def _natten_bias_table(rpb, rows):
    W = GRID_W
    kh, kw = min(WIN_H, rows), min(WIN_W, W)
    c = np.arange(W)[:, None]
    kc = np.arange(W)[None, :]
    cs = np.clip(c - kw // 2, 0, W - kw)
    col_ok = (kc >= cs) & (kc < cs + kw)
    col_hot = (col_ok[..., None] & ((kc - c + (WIN_W - 1))[..., None] == np.arange(2 * WIN_W - 1))).astype(np.float32)
    tabs = []
    for r0, bs in ((0, 0), (NA_QROWS, 0), (rows - NA_QROWS, rows - NA_BAND)):
        r = r0 + np.arange(NA_QROWS)[:, None]
        kr = bs + np.arange(NA_BAND)[None, :]
        rs = np.clip(r - kh // 2, 0, rows - kh)
        row_ok = (kr >= rs) & (kr < rs + kh)
        row_hot = (row_ok[..., None] & ((kr - r + (WIN_H - 1))[..., None] == np.arange(2 * WIN_H - 1))).astype(np.float32)
        t = jnp.einsum('rkd,cje,hde->hrckj', row_hot, col_hot, rpb, precision=HIGHEST)
        ok = row_ok[:, None, :, None] & col_ok[None, :, None, :]
        t = jnp.where(ok[None], t, MASK_VALUE)
        tabs.append(t.reshape(rpb.shape[0], NA_QROWS * W, NA_BAND * W))
    return jnp.stack(tabs, axis=1)


def natten(u, uc, q_g, k_g, rpb):
    B, T, _ = u.shape
    L = uc.shape[1]
    rows = T // GRID_W
    assert rows % NA_QROWS == 0 and rows >= NA_BAND and NA_BAND == 3 * NA_QROWS
    nb = rows // NA_QROWS
    nq = NA_QROWS * GRID_W
    P = NA_HEADS * NA_DIM // LANES
    table = _natten_bias_table(rpb.astype(f32), rows)

    def band(base, j):
        return pl.BlockSpec((1, nq, LANES), lambda b, q, i: (b, jnp.clip(i - 1, 0, nb - 3) + j, base + q))

    ctx = lambda base: pl.BlockSpec((1, L, LANES), lambda b, q, i: (b, 0, base + q))
    cls = lambda i: jnp.where(i == 0, 0, jnp.where(i == nb - 1, 2, 1))
    bias = lambda h: pl.BlockSpec((1, 1, nq, 3 * nq), lambda b, q, i: (2 * q + h, cls(i), 0, 0))
    vec = pl.BlockSpec((1, LANES), lambda b, q, i: (0, 0))
    two = lambda g: jnp.tile(g, 2)[None]
    return pl.pallas_call(
        _natten_kernel, grid=(B, P, nb),
        in_specs=[pl.BlockSpec((1, nq, LANES), lambda b, q, i: (b, i, q)),
                  band(P, 0), band(P, 1), band(P, 2), band(2 * P, 0), band(2 * P, 1), band(2 * P, 2),
                  ctx(P), ctx(2 * P), bias(0), bias(1), vec, vec],
        out_specs=pl.BlockSpec((1, nq, LANES), lambda b, q, i: (b, i, q)),
        out_shape=jax.ShapeDtypeStruct((B, T, NA_HEADS * NA_DIM), f32),
        compiler_params=_cparams(("parallel", "parallel", "arbitrary")),
        name="natten",
    )(u, u, u, u, u, u, u, uc, uc, table, table, two(q_g), two(k_g))


def _moe_ffn_kernel(h_ref, idx_ref, gate_ref, w1_ref, w3_ref, w2_ref, o_ref):
    C = idx_ref.shape[2]
    T = h_ref.shape[1]
    tok = lax.broadcasted_iota(jnp.int32, (C, T), 1)
    onehot = jnp.where(tok == idx_ref[0, 0], 1.0, 0.0).astype(bf16)
    xs = jnp.dot(onehot, h_ref[0], preferred_element_type=f32).astype(bf16)
    a1 = jnp.dot(xs, w1_ref[0], preferred_element_type=f32)
    a3 = jnp.dot(xs, w3_ref[0], preferred_element_type=f32)
    hid = (a1 * jax.nn.sigmoid(a1) * a3).astype(bf16)
    ys = jnp.dot(hid, w2_ref[0], preferred_element_type=f32) * gate_ref[0, 0]
    o_ref[0, 0] = ys.astype(o_ref.dtype)


def moe_ffn(h, idx, gate, w1, w3, w2, *, expert_major):
    B, T, D = h.shape
    E, C = idx.shape[1], idx.shape[2]
    F = w1.shape[2]
    if expert_major:
        grid, be = (E, B), (lambda e, b: (b, e))
    else:
        grid, be = (B, E), (lambda b, e: (b, e))
    bmap = lambda *g: (be(*g)[0], 0, 0)
    emap = lambda *g: (be(*g)[1], 0, 0)
    bemap = lambda *g: (*be(*g), 0, 0)
    return pl.pallas_call(
        _moe_ffn_kernel, grid=grid,
        in_specs=[pl.BlockSpec((1, T, D), bmap),
                  pl.BlockSpec((1, 1, C, 1), bemap),
                  pl.BlockSpec((1, 1, C, 1), bemap),
                  pl.BlockSpec((1, D, F), emap),
                  pl.BlockSpec((1, D, F), emap),
                  pl.BlockSpec((1, F, D), emap)],
        out_specs=pl.BlockSpec((1, 1, C, D), bemap),
        out_shape=jax.ShapeDtypeStruct((B, E, C, D), bf16),
        compiler_params=_cparams(("parallel", "arbitrary")),
        name="moe_ffn",
    )(h, idx[..., None], gate[..., None].astype(f32), w1, w3, w2)


def _moe_scatter_kernel(ys_ref, idx_ref, x_ref, g_ref, o_ref, *, chunk):
    tt = x_ref.shape[1]
    EC = ys_ref.shape[1]
    t0 = pl.program_id(2) * tt
    tok = lax.broadcasted_iota(jnp.int32, (tt, chunk), 0) + t0
    acc = jnp.zeros(o_ref.shape[1:], f32)
    for s in range(EC // chunk):
        onehot = jnp.where(tok == idx_ref[0, :, s * chunk:(s + 1) * chunk], 1.0, 0.0).astype(bf16)
        acc = acc + jnp.dot(onehot, ys_ref[0, s * chunk:(s + 1) * chunk, :], preferred_element_type=f32)
    o_ref[0] = x_ref[0] + g_ref[0] * acc


def moe_scatter(ys, idx, x, g, *, tt=512, dn=512):
    B, E, C, D = ys.shape
    T = x.shape[1]
    EC = E * C
    tt = _pick_tile(T, tt, 8)
    dn = _pick_tile(D, dn, LANES)
    chunk = _pick_tile(EC, 512, LANES)
    gmap = (lambda b, d, i: (b, 0, d)) if g.shape[0] > 1 else (lambda b, d, i: (0, 0, d))
    return pl.pallas_call(
        functools.partial(_moe_scatter_kernel, chunk=chunk), grid=(B, D // dn, T // tt),
        in_specs=[pl.BlockSpec((1, EC, dn), lambda b, d, i: (b, 0, d)),
                  pl.BlockSpec((1, 1, EC), lambda b, d, i: (b, 0, 0)),
                  pl.BlockSpec((1, tt, dn), lambda b, d, i: (b, i, d)),
                  pl.BlockSpec((1, 1, dn), gmap)],
        out_specs=pl.BlockSpec((1, tt, dn), lambda b, d, i: (b, i, d)),
        out_shape=jax.ShapeDtypeStruct((B, T, D), f32),
        compiler_params=_cparams(("parallel", "parallel", "arbitrary")),
        name="moe_scatter",
    )(ys.reshape(B, EC, D), idx.reshape(B, 1, EC), x, g.astype(f32))


def expert_choice_ffn(x, scale, shift, g, router, w1, w3, w2, *, expert_major):
    B, T, D = x.shape
    E = router.shape[1]
    cap = max(1, EC_FACTOR * T // E)
    router_p = jnp.pad(router, ((0, 0), (0, LANES - E))).astype(bf16)
    logits, h = linear(x, router_p, norm=True, scale=scale, shift=shift, emit_x=True)
    aff = jax.nn.softmax(logits[..., :E], axis=-1)
    gate, idx = lax.top_k(jnp.swapaxes(aff, 1, 2), cap)
    ys = moe_ffn(h, idx, gate, w1, w3, w2, expert_major=expert_major)
    return moe_scatter(ys, idx, x, g)


def _rms(x, g):
    return x * lax.rsqrt(jnp.mean(jnp.square(x), -1, keepdims=True) + EPS) * g


def _rope_tables(n_tok, rot_dim):
    t = jnp.arange(n_tok, dtype=jnp.int32)
    row = (t // GRID_W).astype(f32)
    col = (t % GRID_W).astype(f32)
    half = rot_dim // 2
    inv = ROPE_BASE ** (-jnp.arange(0, half, 2, dtype=f32) / half)
    ar = row[:, None] * inv
    ac = col[:, None] * inv
    ang = jnp.concatenate([ar, ar, ac, ac], -1)
    return jnp.cos(ang), jnp.sin(ang)


def even_mixer(x, ctx, mods, p, rope_cs, with_ctx_out):
    (scale, shift, gate), (cscale, cshift, cgate) = mods
    cut = (MLA_IN_PAD, 2 * CONV_CH)
    u_mla, u_cv = linear(x, p['w_in'], norm=True, scale=scale, shift=shift, splits=cut)
    uc_mla, uc_cv = linear(ctx, p['w_in'], norm=True, scale=cscale, shift=cshift, splits=cut)
    q, k, v = mla_prep(u_mla, p, rope_cs)
    qc, kc, vc = mla_prep(uc_mla, p, None)
    o_att = attention(q, [(k, v), (kc, vc)])
    post = (p['dw_b'], p['ln_g'], p['ln_b'])
    o_conv = dwconv(u_cv, p['dw_w'], glu=True, post=post)
    half = MLA_HEADS * MLA_V
    out = lambda res, g, oa, oc: linear(oa, p['w_out'][:half], x2=oc, w2=p['w_out'][half:], res=res, gate=g)
    x_new = out(x, gate, o_att, o_conv)
    if not with_ctx_out:
        return x_new, None
    oc_att = attention(qc, [(kc, vc)])
    oc_conv = dwconv(uc_cv, p['dw_w'], glu=True, post=post)
    return x_new, out(ctx, cgate, oc_att, oc_conv)


def _natten_ctx_qkv(u, p):
    B, T, _ = u.shape
    q, k, v = [t.reshape(B, T, NA_HEADS, NA_DIM) for t in jnp.split(u, 3, -1)]
    q = _rms(q, p['na_q_g']) * (NA_DIM ** -0.5)
    k = _rms(k, p['na_k_g'])
    z = jnp.zeros((B, T, NA_HEADS, LANES - NA_DIM), f32)
    ones_lane = z.at[..., 0].set(1.0)
    q, k, v = (jnp.concatenate([t, pad], -1) for t, pad in ((q, z), (k, z), (v, ones_lane)))
    return tuple(jnp.swapaxes(t, 1, 2).astype(bf16) for t in (q, k, v))


def odd_mixer(x, ctx, mods, p, with_ctx_out):
    (scale, shift, gate), (cscale, cshift, cgate) = mods
    cut = (RW_IN, NA_IN)
    u_rw, u_na = linear(x, p['w_in'], norm=True, scale=scale, shift=shift, splits=cut)
    uc_rw, uc_na = linear(ctx, p['w_in'], norm=True, scale=cscale, shift=cshift, splits=cut)
    y_rw, yc_rw = rwkv_mixer(u_rw, uc_rw, p, with_ctx_out)
    y_na = natten(u_na, uc_na, p['na_q_g'], p['na_k_g'], p['rpb'])
    out = lambda res, g, a, b: linear(a, p['w_out'][:RW_DIM], x2=b, w2=p['w_out'][RW_DIM:], res=res, gate=g)
    x_new = out(x, gate, y_rw, y_na)
    if not with_ctx_out:
        return x_new, None
    qc, kc, vc = _natten_ctx_qkv(uc_na, p)
    return x_new, out(ctx, cgate, yc_rw, attention(qc, [(kc, vc)]))


def _lora_by_direction(w):
    z = jnp.zeros_like(w[0])
    return jnp.stack([jnp.concatenate([w[0], z], 0), jnp.concatenate([z, w[1]], 0)]).astype(bf16)


def kernel(x, c, ctx, c_ctx, ada_w, ada_b, norm1_g, norm2_g, ev_w_in, ev_w_out, mla_q_norm, mla_w_uq, mla_kv_norm, mla_w_ukv, mla_q_g, mla_k_g, cv_dw_w, cv_dw_b, cv_ln_g, cv_ln_b, od_w_in, od_w_out, rw_shift_w, rw_w0, rw_w2, rw_a0, rw_a2, rw_g2, rw_k_k, rw_k_a, rw_r_k, rw_gn_g, rw_gn_b, na_q_g, na_k_g, na_rpb, moe_router, moe_w1, moe_w3, moe_w2):
    B, T, D = x.shape
    depth = ada_w.shape[0]
    rope_cs = _rope_tables(T, MLA_ROPE)
    cond = jax.nn.silu(jnp.concatenate([c, c_ctx[None]], 0))
    cond = jnp.pad(cond, ((0, (-(B + 1)) % 8), (0, 0)))[None]
    for i in range(depth):
        last = i == depth - 1
        j = i // 2
        mod = linear(cond, ada_w[i].astype(bf16), bias=ada_b[i][None])[0]
        sh1, sc1, g1, sh2, sc2, g2 = (t[:, None, :] for t in jnp.split(mod[:B], 6, -1))
        csh1, csc1, cg1, csh2, csc2, cg2 = (t[None] for t in jnp.split(mod[B:B + 1], 6, -1))
        n1, n2 = norm1_g[i], norm2_g[i]
        mods = ((n1 * (1.0 + sc1), sh1, g1), (n1 * (1.0 + csc1), csh1, cg1))
        if i % 2 == 0:
            w_in_mla, wq_blocks, wkv_blocks = _mla_weight_blocks(ev_w_in[j], mla_w_uq[j], mla_w_ukv[j])
            w_in = jnp.concatenate([w_in_mla, ev_w_in[j][:, MLA_IN:]], 1)
            p = dict(w_in=w_in.astype(bf16), w_out=ev_w_out[j].astype(bf16), q_norm=mla_q_norm[j],
                     wq_blocks=wq_blocks, kv_norm=mla_kv_norm[j], wkv_blocks=wkv_blocks,
                     q_g=mla_q_g[j], k_g=mla_k_g[j], dw_w=cv_dw_w[j], dw_b=cv_dw_b[j], ln_g=cv_ln_g[j], ln_b=cv_ln_b[j])
            x, ctx_mix = even_mixer(x, ctx, mods, p, rope_cs, not last)
        else:
            p = dict(w_in=od_w_in[j].astype(bf16), w_out=od_w_out[j].astype(bf16), shift_w=rw_shift_w[j],
                     w0=rw_w0[j], w2cat=_lora_by_direction(rw_w2[j]), a0=rw_a0[j], a2cat=_lora_by_direction(rw_a2[j]),
                     g2=rw_g2[j].astype(bf16), k_k=rw_k_k[j], k_a=rw_k_a[j], r_k=rw_r_k[j],
                     gn_g=rw_gn_g[j], gn_b=rw_gn_b[j], na_q_g=na_q_g[j], na_k_g=na_k_g[j], rpb=na_rpb[j])
            x, ctx_mix = odd_mixer(x, ctx, mods, p, not last)
        w1, w3, w2 = (w[i].astype(bf16) for w in (moe_w1, moe_w3, moe_w2))
        x = expert_choice_ffn(x, n2 * (1.0 + sc2), sh2, g2, moe_router[i], w1, w3, w2, expert_major=False)
        if not last:
            ctx = expert_choice_ffn(ctx_mix, n2 * (1.0 + csc2), csh2, cg2, moe_router[i], w1, w3, w2, expert_major=True)
    return x
```

```python
import functools
import math

import numpy as np
import jax
import jax.numpy as jnp
from jax import lax
from jax.experimental import pallas as pl
from jax.experimental.pallas import tpu as pltpu

f32 = jnp.float32
bf16 = jnp.bfloat16
HIGHEST = lax.Precision.HIGHEST

D_MODEL = 1024
GRID_W = 64
ROPE_BASE = 10000.0
EPS = 1e-6
LANES = 128

MLA_HEADS = D_MODEL // 128
MLA_NOPE = 64
MLA_ROPE = 32
MLA_QK = MLA_NOPE + MLA_ROPE
MLA_V = 64
Q_LORA = 3 * D_MODEL // 8
KV_LORA = D_MODEL // 4
MLA_IN = Q_LORA + KV_LORA + MLA_ROPE
MLA_IN_PAD = 768
CONV_CH = D_MODEL // 2
CONV_K = 31

RW_N = 64
RW_HEADS = D_MODEL // 128
RW_DIM = RW_HEADS * RW_N
DECAY_LORA = 64
ICLR_LORA = 64
GATE_LORA = 128
SHIFT_K = 3
RW_IN = 3 * RW_DIM + 2 * DECAY_LORA + 2 * ICLR_LORA + GATE_LORA
GN_EPS = 64e-5

NA_HEADS = D_MODEL // 128
NA_DIM = 64
WIN_H = 8
WIN_W = 16
NA_IN = 3 * NA_HEADS * NA_DIM
NA_QROWS = 4
NA_BAND = 12

N_EXPERTS = 16
EC_FACTOR = 2

WKV_CHUNK = 64
WKV_INV_BASE = 16
WKV_CHUNKS_PER_STEP = 2
WKV_SCAN_CHUNKS_PER_STEP = 4
CONV_HALO = 32
MASK_VALUE = -1e30

VMEM_LIMIT = 56 * 1024 * 1024


def _cparams(sem):
    return pltpu.CompilerParams(dimension_semantics=sem, vmem_limit_bytes=VMEM_LIMIT)


def _mm(a, b):
    return jnp.dot(a.astype(bf16), b.astype(bf16), preferred_element_type=f32)


def _mm_nt(a, b):
    return lax.dot_general(a.astype(bf16), b.astype(bf16), (((1,), (1,)), ((), ())), preferred_element_type=f32)


def _mm_tn(a, b):
    return lax.dot_general(a.astype(bf16), b.astype(bf16), (((0,), (0,)), ((), ())), preferred_element_type=f32)


def _head_masks():
    lane = lax.broadcasted_iota(jnp.int32, (1, LANES), 1)
    m0 = jnp.where(lane < LANES // 2, 1.0, 0.0).astype(f32)
    return m0, 1.0 - m0


def _head_sum(x, masks):
    s0 = jnp.sum(x * masks[0], axis=-1, keepdims=True)
    s1 = jnp.sum(x * masks[1], axis=-1, keepdims=True)
    return s0 * masks[0] + s1 * masks[1]


def _linear_kernel(*refs, norm, has_scale, has_shift, has_x2, has_bias, has_res, emit_x, splits):
    it = iter(refs)
    x_ref, w_ref = next(it), next(it)
    scale_ref = next(it) if has_scale else None
    shift_ref = next(it) if has_shift else None
    x2_ref = next(it) if has_x2 else None
    w2_ref = next(it) if has_x2 else None
    bias_ref = next(it) if has_bias else None
    res_ref = next(it) if has_res else None
    gate_ref = next(it) if has_res else None
    o_refs = [next(it) for _ in splits]
    xo_ref = next(it) if emit_x else None
    xb_ref = next(it)

    @pl.when(pl.program_id(2) == 0)
    def _():
        x = x_ref[0].astype(f32)
        if norm:
            x = x * lax.rsqrt(jnp.mean(x * x, axis=-1, keepdims=True) + EPS)
        if has_scale:
            x = x * scale_ref[0]
        if has_shift:
            x = x + shift_ref[0]
        xb_ref[...] = x.astype(bf16)
        if emit_x:
            xo_ref[0] = xb_ref[...]

    lo = 0
    for o_ref, width in zip(o_refs, splits):
        cols = slice(lo, lo + width) if len(splits) > 1 else slice(None)
        acc = jnp.dot(xb_ref[...], w_ref[:, cols], preferred_element_type=f32)
        if has_x2:
            acc = acc + jnp.dot(x2_ref[0].astype(bf16), w2_ref[:, cols], preferred_element_type=f32)
        if has_bias:
            acc = acc + bias_ref[:, cols]
        if has_res:
            acc = res_ref[0] + gate_ref[0] * acc
        o_ref[0] = acc.astype(o_ref.dtype)
        lo += width


def _pick_tile(n, target, align):
    if n <= target:
        return n
    t = (target // align) * align
    while t > align and n % t:
        t -= align
    assert n % t == 0, (n, target, align)
    return t


def linear(x, w, *, norm=False, scale=None, shift=None, x2=None, w2=None, bias=None,
           res=None, gate=None, emit_x=False, splits=None, tm=512, tn=None, out_dtype=f32):
    B, T, K = x.shape
    N = w.shape[1]
    assert w.shape[0] == K and N % LANES == 0, (x.shape, w.shape)
    tm = _pick_tile(T, tm, 8)
    if splits is None:
        tn = _pick_tile(N, 2048 if tn is None else tn, LANES)
        widths = (tn,)
    else:
        assert sum(splits) == N and all(s % LANES == 0 for s in splits) and res is None
        tn, widths = N, tuple(splits)
    grid = (B, T // tm, N // tn)

    def bvec(a):
        return (lambda b, i, j: (b, 0, 0)) if a.shape[0] > 1 else (lambda b, i, j: (0, 0, 0))

    args = [x, w]
    specs = [pl.BlockSpec((1, tm, K), lambda b, i, j: (b, i, 0)),
             pl.BlockSpec((K, tn), lambda b, i, j: (0, j))]
    if scale is not None:
        args.append(scale.astype(f32))
        specs.append(pl.BlockSpec((1, 1, K), bvec(scale)))
    if shift is not None:
        args.append(shift.astype(f32))
        specs.append(pl.BlockSpec((1, 1, K), bvec(shift)))
    if x2 is not None:
        K2 = x2.shape[-1]
        args += [x2, w2]
        specs += [pl.BlockSpec((1, tm, K2), lambda b, i, j: (b, i, 0)),
                  pl.BlockSpec((K2, tn), lambda b, i, j: (0, j))]
    if bias is not None:
        args.append(bias.astype(f32))
        specs.append(pl.BlockSpec((1, tn), lambda b, i, j: (0, j)))
    if res is not None:
        args += [res, gate.astype(f32)]
        specs += [pl.BlockSpec((1, tm, tn), lambda b, i, j: (b, i, j)),
                  pl.BlockSpec((1, 1, tn), (lambda b, i, j: (b, 0, j)) if gate.shape[0] > 1
                               else (lambda b, i, j: (0, 0, j)))]
    if splits is None:
        out_shape = [jax.ShapeDtypeStruct((B, T, N), out_dtype)]
        out_specs = [pl.BlockSpec((1, tm, tn), lambda b, i, j: (b, i, j))]
    else:
        out_shape = [jax.ShapeDtypeStruct((B, T, s), out_dtype) for s in splits]
        out_specs = [pl.BlockSpec((1, tm, s), lambda b, i, j: (b, i, 0)) for s in splits]
    if emit_x:
        out_shape.append(jax.ShapeDtypeStruct((B, T, K), bf16))
        out_specs.append(pl.BlockSpec((1, tm, K), lambda b, i, j: (b, i, 0)))
    kern = functools.partial(_linear_kernel, norm=norm, has_scale=scale is not None,
                             has_shift=shift is not None, has_x2=x2 is not None,
                             has_bias=bias is not None, has_res=res is not None, emit_x=emit_x, splits=widths)
    outs = pl.pallas_call(
        kern, grid=grid, in_specs=specs, out_specs=out_specs, out_shape=out_shape,
        scratch_shapes=[pltpu.VMEM((tm, K), bf16)],
        compiler_params=_cparams(("parallel", "parallel", "arbitrary")),
        name="linear",
    )(*args)
    return outs if (emit_x or splits is not None) else outs[0]


ATT_DV = LANES // 2


def _attn_kernel(*refs):
    q_ref, o_ref = refs[0], refs[-1]
    kv = refs[1:-1]
    lane = lax.broadcasted_iota(jnp.int32, (1, LANES), 1)
    res = []
    for h in range(2):
        q = q_ref[0, h]
        s = [_mm_nt(q, k_ref[0, h]) for k_ref in kv[0::2]]
        m = functools.reduce(jnp.maximum, [jnp.max(x, axis=-1, keepdims=True) for x in s])
        o = functools.reduce(jnp.add, [jnp.dot(jnp.exp((x - m).astype(bf16)), v_ref[0, h], preferred_element_type=f32)
                                       for x, v_ref in zip(s, kv[1::2])])
        res.append(o / o[:, ATT_DV:ATT_DV + 1])
    o_ref[0] = jnp.where(lane < ATT_DV, res[0], pltpu.roll(res[1], ATT_DV, axis=1))


def attention(q, kvs, *, tq=256):
    B, H, T, _ = q.shape
    assert H % 2 == 0
    tq = _pick_tile(T, tq, 8)
    args, specs = [q], [pl.BlockSpec((1, 2, tq, LANES), lambda b, h, i: (b, h, i, 0))]
    for k, v in kvs:
        args += [k, v]
        specs += [pl.BlockSpec((1, 2, k.shape[2], LANES), lambda b, h, i: (b, h, 0, 0))] * 2
    return pl.pallas_call(
        _attn_kernel, grid=(B, H // 2, T // tq), in_specs=specs,
        out_specs=pl.BlockSpec((1, tq, LANES), lambda b, h, i: (b, i, h)),
        out_shape=jax.ShapeDtypeStruct((B, T, H * ATT_DV), f32),
        compiler_params=_cparams(("parallel", "parallel", "arbitrary")),
        name="attention",
    )(*args)


def _mla_prep_kernel(u_ref, qn_ref, kvn_ref, wq_ref, wkv_ref, cq_ref, sq_ref, ck_ref, sk_ref,
                     q_ref, k_ref, v_ref):
    u = u_ref[0]
    H = q_ref.shape[1]

    def low_rank_norm(x, g_ref):
        return (x * lax.rsqrt(jnp.mean(x * x, axis=-1, keepdims=True) + EPS) * g_ref[...]).astype(bf16)

    qa = jnp.dot(low_rank_norm(u[:, :Q_LORA], qn_ref), wq_ref[...], preferred_element_type=f32)
    kva = jnp.dot(low_rank_norm(u[:, Q_LORA:Q_LORA + KV_LORA], kvn_ref), wkv_ref[...], preferred_element_type=f32)
    shared = pltpu.roll(u[:, Q_LORA + KV_LORA:], MLA_NOPE, axis=1)
    lane = lax.broadcasted_iota(jnp.int32, (1, LANES), 1)
    real = lane < MLA_QK
    ones_lane = jnp.where(lane == ATT_DV, 1.0, 0.0)

    def head_norm_rope(x, cos_ref, sin_ref, scale):
        ms = jnp.sum(jnp.where(real, x * x, 0.0), axis=-1, keepdims=True) * (1.0 / MLA_QK)
        y = x * cos_ref[...] + pltpu.roll(x, LANES - MLA_ROPE, axis=1) * sin_ref[...]
        return (y * (lax.rsqrt(ms + EPS) * scale)).astype(bf16)

    for h in range(H):
        cols = slice(h * LANES, (h + 1) * LANES)
        q_ref[0, h] = head_norm_rope(qa[:, cols], cq_ref, sq_ref, MLA_QK ** -0.5)
        k_ref[0, h] = head_norm_rope(kva[:, cols] + shared, ck_ref, sk_ref, 1.0)
        v_ref[0, h] = (kva[:, H * LANES + h * LANES:H * LANES + (h + 1) * LANES] + ones_lane).astype(bf16)


_ROPE_PERM = np.array([8, 9, 10, 11, 12, 13, 14, 15, 0, 1, 2, 3, 4, 5, 6, 7,
                       24, 25, 26, 27, 28, 29, 30, 31, 16, 17, 18, 19, 20, 21, 22, 23])
_ROPE_SIGN = np.array([-1.0] * 8 + [1.0] * 8 + [-1.0] * 8 + [1.0] * 8, np.float32)


def _mla_tables(g, rope_cs, n_tok):
    if rope_cs is None:
        cos, sin = jnp.ones((n_tok, MLA_ROPE), f32), jnp.zeros((n_tok, MLA_ROPE), f32)
    else:
        cos, sin = rope_cs
    pad = jnp.zeros((n_tok, LANES - MLA_QK), f32)
    c = jnp.concatenate([jnp.broadcast_to(g[:MLA_NOPE], (n_tok, MLA_NOPE)), g[MLA_NOPE:] * cos, pad], -1)
    s = jnp.concatenate([jnp.zeros((n_tok, MLA_NOPE), f32), g[MLA_NOPE:][_ROPE_PERM] * sin, pad], -1)
    return c, s


def mla_prep(u, p, rope_cs, *, tm=256):
    B, T, W = u.shape
    H = MLA_HEADS
    tm = _pick_tile(T, tm, 8)
    cq, sq = _mla_tables(p['q_g'], rope_cs, T)
    ck, sk = _mla_tables(p['k_g'], rope_cs, T)
    row = lambda a: pl.BlockSpec((1, a.shape[1]), lambda b, i: (0, 0))
    tab = pl.BlockSpec((tm, LANES), lambda b, i: (i, 0))
    full = lambda a: pl.BlockSpec(a.shape, lambda b, i: (0, 0))
    out = pl.BlockSpec((1, H, tm, LANES), lambda b, i: (b, 0, i, 0))
    shape = jax.ShapeDtypeStruct((B, H, T, LANES), bf16)
    qn, kvn = p['q_norm'][None], p['kv_norm'][None]
    return pl.pallas_call(
        _mla_prep_kernel, grid=(B, T // tm),
        in_specs=[pl.BlockSpec((1, tm, W), lambda b, i: (b, i, 0)), row(qn), row(kvn),
                  full(p['wq_blocks']), full(p['wkv_blocks']), tab, tab, tab, tab],
        out_specs=[out, out, out], out_shape=[shape, shape, shape],
        compiler_params=_cparams(("parallel", "parallel")),
        name="mla_prep",
    )(u, qn, kvn, p['wq_blocks'], p['wkv_blocks'], cq, sq, ck, sk)


def _mla_weight_blocks(w_in, w_uq, w_ukv):
    H = MLA_HEADS
    partner = lambda w: w[..., MLA_NOPE + _ROPE_PERM] * _ROPE_SIGN
    d = w_in.shape[0]
    kr = w_in[:, Q_LORA + KV_LORA:MLA_IN]
    w_in_mla = jnp.concatenate([w_in[:, :MLA_IN], kr[:, _ROPE_PERM] * _ROPE_SIGN,
                                jnp.zeros((d, MLA_IN_PAD - MLA_IN - MLA_ROPE), w_in.dtype)], 1)
    wq = w_uq.reshape(Q_LORA, H, MLA_QK)
    wq = jnp.concatenate([wq, partner(wq)], -1).reshape(Q_LORA, H * LANES)
    wkv = w_ukv.reshape(KV_LORA, H, MLA_NOPE + MLA_V)
    z = jnp.zeros((KV_LORA, H, LANES - MLA_NOPE), w_ukv.dtype)
    wk = jnp.concatenate([wkv[..., :MLA_NOPE], z], -1).reshape(KV_LORA, H * LANES)
    wv = jnp.concatenate([wkv[..., MLA_NOPE:], z], -1).reshape(KV_LORA, H * LANES)
    return w_in_mla, wq.astype(bf16), jnp.concatenate([wk, wv], 1).astype(bf16)


def _dwconv_kernel(*refs, taps, tb, C, glu, post):
    it = iter(refs)
    xp_ref, xm_ref, xn_ref, w_ref = next(it), next(it), next(it), next(it)
    if post:
        b_ref, g_ref, bb_ref = next(it), next(it), next(it)
    o_ref = next(it)
    hs_ref = next(it)
    i = pl.program_id(1)

    def pre(x):
        if glu:
            return x[:, :C] * jax.nn.sigmoid(x[:, C:])
        return x

    hs_ref[0:CONV_HALO, :] = pre(xp_ref[0]) * jnp.where(i > 0, 1.0, 0.0)
    hs_ref[CONV_HALO:CONV_HALO + tb, :] = pre(xm_ref[0])
    hs_ref[CONV_HALO + tb:, :] = pre(xn_ref[0]) * jnp.where(i < pl.num_programs(1) - 1, 1.0, 0.0)
    off = CONV_HALO - (taps - 1) // 2
    acc = jnp.zeros((tb, C), f32)
    for j in range(taps):
        acc = acc + w_ref[j:j + 1, :] * hs_ref[off + j:off + j + tb, :]
    if post:
        acc = acc + b_ref[...]
        mu = jnp.mean(acc, axis=-1, keepdims=True)
        d = acc - mu
        var = jnp.mean(d * d, axis=-1, keepdims=True)
        y = d * lax.rsqrt(var + EPS) * g_ref[...] + bb_ref[...]
        acc = y * jax.nn.sigmoid(y)
    o_ref[0] = acc


def dwconv(x, w, *, glu=False, post=None, tb=256):
    B, T, Cin = x.shape
    taps, C = w.shape
    assert Cin == (2 * C if glu else C) and (taps - 1) // 2 <= CONV_HALO
    tb = _pick_tile(T, tb, CONV_HALO)
    assert tb % CONV_HALO == 0
    r, nh = tb // CONV_HALO, T // CONV_HALO
    args = [x, x, x, w]
    specs = [pl.BlockSpec((1, CONV_HALO, Cin), lambda b, i: (b, jnp.maximum(i * r - 1, 0), 0)),
             pl.BlockSpec((1, tb, Cin), lambda b, i: (b, i, 0)),
             pl.BlockSpec((1, CONV_HALO, Cin), lambda b, i: (b, jnp.minimum((i + 1) * r, nh - 1), 0)),
             pl.BlockSpec((taps, C), lambda b, i: (0, 0))]
    if post is not None:
        args += [p.reshape(1, C) for p in post]
        specs += [pl.BlockSpec((1, C), lambda b, i: (0, 0))] * 3
    kern = functools.partial(_dwconv_kernel, taps=taps, tb=tb, C=C, glu=glu, post=post is not None)
    return pl.pallas_call(
        kern, grid=(B, T // tb), in_specs=specs,
        out_specs=pl.BlockSpec((1, tb, C), lambda b, i: (b, i, 0)),
        out_shape=jax.ShapeDtypeStruct((B, T, C), f32),
        scratch_shapes=[pltpu.VMEM((tb + 2 * CONV_HALO, C), f32)],
        compiler_params=_cparams(("parallel", "parallel")),
        name="dwconv",
    )(*args)


def _bmm(a, b):
    return jnp.einsum('gij,gjk->gik', a.astype(bf16), b.astype(bf16), preferred_element_type=f32)


def _bmm_nt(a, b):
    return jnp.einsum('gik,gjk->gij', a.astype(bf16), b.astype(bf16), preferred_element_type=f32)


def _bmm_tn(a, b):
    return lax.dot_general(a.astype(bf16), b.astype(bf16), (((1,), (1,)), ((0,), (0,))),
                           preferred_element_type=f32)


def _wkv_units(r, k_eff, v, lw, a, b, sgn, masks):
    G, Tc, _ = r.shape
    ti = lax.broadcasted_iota(jnp.int32, (G, Tc, Tc), 1)
    tj = lax.broadcasted_iota(jnp.int32, (G, Tc, Tc), 2)
    tri = jnp.where((tj - ti) * sgn <= 0, 1.0, 0.0).astype(bf16)
    hi = lw.astype(bf16)
    rest = lw - hi.astype(f32)
    mid = rest.astype(bf16)
    lo = (rest - mid.astype(f32)).astype(bf16)
    L = _bmm(tri, hi) + _bmm(tri, mid) + _bmm(tri, lo)
    Ltot = jnp.sum(lw, axis=1, keepdims=True)
    enL = jnp.exp(-L)
    eR = jnp.exp(Ltot - L)
    At, Rt, Bt, Kt, Bb, Kb = a * jnp.exp(L - lw), r * jnp.exp(L), b * enL, k_eff * enL, b * eR, k_eff * eR

    stack = lambda x: jnp.concatenate([x * masks[0], x * masks[1]], axis=1)
    Y, X, Xk, Vs = stack(At), stack(Bt), stack(Kt), stack(v)
    n2 = 2 * Tc
    gi = lax.broadcasted_iota(jnp.int32, (G, n2, n2), 1)
    gj = lax.broadcasted_iota(jnp.int32, (G, n2, n2), 2)
    before = ((gj & (Tc - 1)) - (gi & (Tc - 1))) * sgn < 0
    Aab = jnp.where(before, _bmm_nt(Y, X), 0.0)
    Aak = jnp.where(before, _bmm_nt(Y, Xk), 0.0)
    ri = lax.broadcasted_iota(jnp.int32, (G, Tc, 2 * n2), 1)
    rj = lax.broadcasted_iota(jnp.int32, (G, Tc, 2 * n2), 2)
    upto = ((rj & (Tc - 1)) - ri) * sgn <= 0
    RB = jnp.where(upto, _bmm_nt(Rt, jnp.concatenate([X, Xk], axis=1)), 0.0)

    same = lambda size: jnp.right_shift(gi, int(math.log2(size))) == jnp.right_shift(gj, int(math.log2(size)))
    Xp = jnp.where(same(WKV_INV_BASE), Aab, 0.0)
    Tm = jnp.where(gi == gj, 1.0, 0.0) + Xp
    span = 2
    while span < WKV_INV_BASE:
        Xp = _bmm(Xp, Xp)
        Tm = Tm + _bmm(Tm, Xp)
        span *= 2
    size = WKV_INV_BASE
    while size < Tc:
        off = jnp.where(same(2 * size) & jnp.logical_not(same(size)), Aab, 0.0)
        Tm = Tm + _bmm(_bmm(Tm, off), Tm)
        size *= 2

    TA = _bmm(Tm, jnp.concatenate([_bmm(Aak, Vs), Y], axis=2))
    U0, Ah = TA[:, :, :LANES], TA[:, :, LANES:]
    UV = jnp.concatenate([U0, Vs], axis=1)
    Bs = stack(Bb)
    ki = lax.broadcasted_iota(jnp.int32, (G, LANES, LANES), 1)
    kj = lax.broadcasted_iota(jnp.int32, (G, LANES, LANES), 2)
    M = _bmm_tn(Bs, Ah) + jnp.where(ki == kj, jnp.exp(Ltot), 0.0)
    N = _bmm_tn(jnp.concatenate([Bs, stack(Kb)], axis=1), UV)
    return Rt + _bmm(RB[:, :, :n2], Ah), _bmm(RB, UV), M, N


def _wkv_chunk_kernel(r_ref, k_ref, v_ref, lw_ref, la_ref, w2_ref, w0_ref, a2_ref, a0_ref,
                      kk_ref, ka_ref, rk_ref, rh_ref, y0_ref, m_ref, n_ref):
    masks = _head_masks()
    P = r_ref.shape[2] // LANES
    CH = m_ref.shape[2]
    Tc = r_ref.shape[1] // CH
    def units(x):
        rows = (lambda j: slice(j * Tc, (j + 1) * Tc)) if x.shape[0] > 1 else (lambda j: slice(None))
        return jnp.stack([x[rows(j), q * LANES:(q + 1) * LANES] for q in range(P) for j in range(CH)], axis=0)

    lw_in, la_in = jnp.tanh(lw_ref[0]), la_ref[0]
    r, k, v = units(r_ref[0]), units(k_ref[0]), units(v_ref[0])
    kk = k * units(kk_ref[...])
    kk = kk * lax.rsqrt(_head_sum(kk * kk, masks) + 1e-12)
    lw, k_eff, b, bonus = [], [], [], []
    for d in range(2):
        logw = w0_ref[d] + _mm(lw_in, w2_ref[d])
        lw.append(units(-jnp.exp(-jax.nn.softplus(-logw) - 0.5)))
        rate = units(jax.nn.sigmoid(a0_ref[d] + _mm(la_in, a2_ref[d])))
        k_eff.append(k * (1.0 + (rate - 1.0) * units(ka_ref[...])))
        b.append(kk * rate)
        bonus.append(_head_sum(r * k_eff[d] * units(rk_ref[...]), masks) * v)
    both = lambda x: jnp.concatenate([x, x], axis=0)
    cat = lambda xs: jnp.concatenate(xs, axis=0)
    G = P * CH
    unit = lax.broadcasted_iota(jnp.int32, (2 * G, 1, 1), 0)
    sgn = jnp.where(unit < G, 1, -1)
    rh, y0, M, N = _wkv_units(both(r), cat(k_eff), both(v), cat(lw), both(-kk), cat(b), sgn, masks)
    y0 = y0 + cat(bonus)
    for d in range(2):
        for q in range(P):
            for j in range(CH):
                g, rows, cols = d * G + q * CH + j, slice(j * Tc, (j + 1) * Tc), slice(q * LANES, (q + 1) * LANES)
                rh_ref[d, 0, rows, cols] = rh[g].astype(rh_ref.dtype)
                y0_ref[d, 0, rows, cols] = y0[g]
                m_ref[d, 0, j, q] = M[g].astype(m_ref.dtype)
                n_ref[d, 0, j, q] = N[g]


def wkv_chunk(us, p):
    B, T, _ = us.shape
    Tc = WKV_CHUNK
    assert T % Tc == 0 and RW_IN % RW_DIM == 3 * LANES
    nc, P = T // Tc, RW_DIM // LANES
    CH = _pick_tile(nc, WKV_CHUNKS_PER_STEP, 1)
    wide = lambda i: pl.BlockSpec((1, CH * Tc, RW_DIM), lambda b, c: (b, c, i))
    lora_in = lambda i: pl.BlockSpec((1, CH * Tc, LANES), lambda b, c: (b, c, 3 * P + i))
    full = lambda a: pl.BlockSpec(a.shape, lambda b, c: (0,) * a.ndim)
    seq = pl.BlockSpec((2, 1, CH * Tc, RW_DIM), lambda b, c: (0, b, c, 0))
    mat = pl.BlockSpec((2, 1, CH, P, LANES, LANES), lambda b, c: (0, b, c, 0, 0, 0))
    params = [p['w2cat'], p['w0'][:, None, :], p['a2cat'], p['a0'][:, None, :],
              p['k_k'][None], p['k_a'][None], p['r_k'][None]]
    return pl.pallas_call(
        _wkv_chunk_kernel, grid=(B, nc // CH),
        in_specs=[wide(0), wide(1), wide(2), lora_in(0), lora_in(1)] + [full(a) for a in params],
        out_specs=[seq, seq, mat, mat],
        out_shape=[jax.ShapeDtypeStruct((2, B, T, RW_DIM), bf16), jax.ShapeDtypeStruct((2, B, T, RW_DIM), f32),
                   jax.ShapeDtypeStruct((2, B, nc, P, LANES, LANES), bf16),
                   jax.ShapeDtypeStruct((2, B, nc, P, LANES, LANES), f32)],
        compiler_params=_cparams(("parallel", "parallel")),
        name="wkv_chunk",
    )(us, us, us, us, us, *params)


def _wkv_scan_kernel(*refs, final, order):
    it = iter(refs)
    rh_ref, y0_ref, m_ref, n_ref, h0_ref = (next(it) for _ in range(5))
    if final:
        yo_ref, lg_ref, g2_ref, gg_ref, gb_ref = (next(it) for _ in range(5))
    y_ref, ht_ref, h_ref = next(it), next(it), next(it)
    c = pl.program_id(1)
    P = h_ref.shape[0]
    Tc = rh_ref.shape[2] // len(order)

    @pl.when(c == 0)
    def _():
        h_ref[...] = h0_ref[0]

    pairs = lambda x: jnp.stack([x[:, q * LANES:(q + 1) * LANES] for q in range(P)], axis=0)
    H = h_ref[...]
    if final:
        masks = _head_masks()
        gate = _mm(jax.nn.sigmoid(lg_ref[0]), g2_ref[...])
    for j in order:
        rows = slice(j * Tc, (j + 1) * Tc)
        y = pairs(y0_ref[0, 0, rows]) + _bmm(pairs(rh_ref[0, 0, rows]), H)
        H = _bmm(m_ref[0, 0, j], H) + n_ref[0, 0, j]
        if final:
            y = y + pairs(yo_ref[0, rows])
            mu = _head_sum(y, masks) * (1.0 / RW_N)
            dv = y - mu
            var = _head_sum(dv * dv, masks) * (1.0 / RW_N)
            y = (dv * lax.rsqrt(var + GN_EPS) * pairs(gg_ref[...]) + pairs(gb_ref[...])) * pairs(gate[rows])
        for q in range(P):
            y_ref[0, rows, q * LANES:(q + 1) * LANES] = y[q]
    h_ref[...] = H

    @pl.when(c == pl.num_programs(1) - 1)
    def _():
        ht_ref[0] = H


def wkv_scan(parts, h0, direction, final=None):
    rh, y0, m, n = parts
    _, B, T, _ = y0.shape
    Tc = WKV_CHUNK
    nc, P = T // Tc, RW_DIM // LANES
    S = _pick_tile(nc, WKV_SCAN_CHUNKS_PER_STEP, 1)
    ns = nc // S
    order = tuple(reversed(range(S))) if direction == 1 else tuple(range(S))
    ch = (lambda c: ns - 1 - c) if direction == 1 else (lambda c: c)
    seq = pl.BlockSpec((1, 1, S * Tc, RW_DIM), lambda b, c: (direction, b, ch(c), 0))
    mat = pl.BlockSpec((1, 1, S, P, LANES, LANES), lambda b, c: (direction, b, ch(c), 0, 0, 0))
    st = pl.BlockSpec((1, P, LANES, LANES), lambda b, c: (b, 0, 0, 0))
    out = pl.BlockSpec((1, S * Tc, RW_DIM), lambda b, c: (b, ch(c), 0))
    args, specs = [rh, y0, m, n, h0], [seq, seq, mat, mat, st]
    if final is not None:
        y_other, us, p = final
        vec = pl.BlockSpec((1, RW_DIM), lambda b, c: (0, 0))
        args += [y_other, us, p['g2'], p['gn_g'][None], p['gn_b'][None]]
        specs += [out, pl.BlockSpec((1, S * Tc, LANES), lambda b, c: (b, ch(c), RW_IN // LANES - 1)),
                  pl.BlockSpec((GATE_LORA, RW_DIM), lambda b, c: (0, 0)), vec, vec]
    return pl.pallas_call(
        functools.partial(_wkv_scan_kernel, final=final is not None, order=order), grid=(B, ns),
        in_specs=specs, out_specs=[out, st],
        out_shape=[jax.ShapeDtypeStruct((B, T, RW_DIM), f32), jax.ShapeDtypeStruct((B, P, LANES, LANES), f32)],
        scratch_shapes=[pltpu.VMEM((P, LANES, LANES), f32)],
        compiler_params=_cparams(("parallel", "arbitrary")),
        name="wkv_scan",
    )(*args)


def rwkv_mixer(u, uc, p, with_ctx_out):
    B = u.shape[0]
    us, usc = dwconv(u, p['shift_w']), dwconv(uc, p['shift_w'])
    lat, cx = wkv_chunk(us, p), wkv_chunk(usc, p)
    zero = jnp.zeros((B, RW_DIM // LANES, LANES, LANES), f32)
    yc_b, h_b = wkv_scan(cx, zero, 1)
    y_b, _ = wkv_scan(lat, h_b, 1)
    yc, h_f = wkv_scan(cx, zero, 0, final=(yc_b, usc, p) if with_ctx_out else None)
    y, _ = wkv_scan(lat, h_f, 0, final=(y_b, us, p))
    return y, (yc if with_ctx_out else None)


def _natten_kernel(q_ref, k0_ref, k1_ref, k2_ref, v0_ref, v1_ref, v2_ref, kc_ref, vc_ref,
                   b0_ref, b1_ref, qg_ref, kg_ref, o_ref):
    masks = _head_masks()
    nq = q_ref.shape[1]

    def norm(x, g):
        return x * lax.rsqrt(_head_sum(x * x, masks) * (1.0 / NA_DIM) + EPS) * g

    q = norm(q_ref[0], qg_ref[...]) * (NA_DIM ** -0.5)
    ks = [norm(r[0], kg_ref[...]).astype(bf16) for r in (k0_ref, k1_ref, k2_ref, kc_ref)]
    vs = [r[0].astype(bf16) for r in (v0_ref, v1_ref, v2_ref, vc_ref)]
    out = jnp.zeros(o_ref.shape[1:], f32)
    for h, bias_ref in enumerate((b0_ref, b1_ref)):
        qh = (q * masks[h]).astype(bf16)
        s = [_mm_nt(qh, ks[j]) + bias_ref[0, 0, :, j * nq:(j + 1) * nq] for j in range(3)]
        s.append(_mm_nt(qh, ks[3]))
        m = functools.reduce(jnp.maximum, [jnp.max(x, axis=-1, keepdims=True) for x in s])
        pr = [jnp.exp(x - m) for x in s]
        l = functools.reduce(jnp.add, [jnp.sum(x, axis=-1, keepdims=True) for x in pr])
        o = functools.reduce(jnp.add, [jnp.dot(pp.astype(bf16), vv, preferred_element_type=f32)
                                       for pp, vv in zip(pr, vs)])
        out = out + (o / l) * masks[h]
    o_ref[0] = out


def _natten_bias_table(rpb, rows):
    W = GRID_W
    kh, kw = min(WIN_H, rows), min(WIN_W, W)
    c = np.arange(W)[:, None]
    kc = np.arange(W)[None, :]
    cs = np.clip(c - kw // 2, 0, W - kw)
    col_ok = (kc >= cs) & (kc < cs + kw)
    col_hot = (col_ok[..., None] & ((kc - c + (WIN_W - 1))[..., None] == np.arange(2 * WIN_W - 1))).astype(np.float32)
    tabs = []
    for r0, bs in ((0, 0), (NA_QROWS, 0), (rows - NA_QROWS, rows - NA_BAND)):
        r = r0 + np.arange(NA_QROWS)[:, None]
        kr = bs + np.arange(NA_BAND)[None, :]
        rs = np.clip(r - kh // 2, 0, rows - kh)
        row_ok = (kr >= rs) & (kr < rs + kh)
        row_hot = (row_ok[..., None] & ((kr - r + (WIN_H - 1))[..., None] == np.arange(2 * WIN_H - 1))).astype(np.float32)
        t = jnp.einsum('rkd,cje,hde->hrckj', row_hot, col_hot, rpb, precision=HIGHEST)
        ok = row_ok[:, None, :, None] & col_ok[None, :, None, :]
        t = jnp.where(ok[None], t, MASK_VALUE)
        tabs.append(t.reshape(rpb.shape[0], NA_QROWS * W, NA_BAND * W))
    return jnp.stack(tabs, axis=1)


def natten(u, uc, q_g, k_g, rpb):
    B, T, _ = u.shape
    L = uc.shape[1]
    rows = T // GRID_W
    assert rows % NA_QROWS == 0 and rows >= NA_BAND and NA_BAND == 3 * NA_QROWS
    nb = rows // NA_QROWS
    nq = NA_QROWS * GRID_W
    P = NA_HEADS * NA_DIM // LANES
    table = _natten_bias_table(rpb.astype(f32), rows)

    def band(base, j):
        return pl.BlockSpec((1, nq, LANES), lambda b, q, i: (b, jnp.clip(i - 1, 0, nb - 3) + j, base + q))

    ctx = lambda base: pl.BlockSpec((1, L, LANES), lambda b, q, i: (b, 0, base + q))
    cls = lambda i: jnp.where(i == 0, 0, jnp.where(i == nb - 1, 2, 1))
    bias = lambda h: pl.BlockSpec((1, 1, nq, 3 * nq), lambda b, q, i: (2 * q + h, cls(i), 0, 0))
    vec = pl.BlockSpec((1, LANES), lambda b, q, i: (0, 0))
    two = lambda g: jnp.tile(g, 2)[None]
    return pl.pallas_call(
        _natten_kernel, grid=(B, P, nb),
        in_specs=[pl.BlockSpec((1, nq, LANES), lambda b, q, i: (b, i, q)),
                  band(P, 0), band(P, 1), band(P, 2), band(2 * P, 0), band(2 * P, 1), band(2 * P, 2),
                  ctx(P), ctx(2 * P), bias(0), bias(1), vec, vec],
        out_specs=pl.BlockSpec((1, nq, LANES), lambda b, q, i: (b, i, q)),
        out_shape=jax.ShapeDtypeStruct((B, T, NA_HEADS * NA_DIM), f32),
        compiler_params=_cparams(("parallel", "parallel", "arbitrary")),
        name="natten",
    )(u, u, u, u, u, u, u, uc, uc, table, table, two(q_g), two(k_g))


def _moe_ffn_kernel(h_ref, idx_ref, gate_ref, w1_ref, w3_ref, w2_ref, o_ref):
    nb, T = h_ref.shape[0], h_ref.shape[1]
    C = idx_ref.shape[2]
    tok = lax.broadcasted_iota(jnp.int32, (C, T), 1)
    xs = [jnp.dot(jnp.where(tok == idx_ref[s, 0], 1.0, 0.0).astype(bf16), h_ref[s],
                  preferred_element_type=f32).astype(bf16) for s in range(nb)]
    xs = jnp.concatenate(xs, axis=0) if nb > 1 else xs[0]
    a1 = jnp.dot(xs, w1_ref[0], preferred_element_type=f32)
    a3 = jnp.dot(xs, w3_ref[0], preferred_element_type=f32)
    hid = (a1 * jax.nn.sigmoid(a1) * a3).astype(bf16)
    ys = jnp.dot(hid, w2_ref[0], preferred_element_type=f32)
    for s in range(nb):
        o_ref[s, 0] = (ys[s * C:(s + 1) * C] * gate_ref[s, 0]).astype(o_ref.dtype)


def moe_ffn(h, idx, gate, w1, w3, w2, *, expert_major):
    B, T, D = h.shape
    E, C = idx.shape[1], idx.shape[2]
    F = w1.shape[2]
    if expert_major:
        nb, grid, be = B, (E, 1), (lambda e, b: (b, e))
    else:
        nb, grid, be = 1, (B, E), (lambda b, e: (b, e))
    bmap = lambda *g: (be(*g)[0], 0, 0)
    emap = lambda *g: (be(*g)[1], 0, 0)
    bemap = lambda *g: (*be(*g), 0, 0)
    return pl.pallas_call(
        _moe_ffn_kernel, grid=grid,
        in_specs=[pl.BlockSpec((nb, T, D), bmap),
                  pl.BlockSpec((nb, 1, C, 1), bemap),
                  pl.BlockSpec((nb, 1, C, 1), bemap),
                  pl.BlockSpec((1, D, F), emap),
                  pl.BlockSpec((1, D, F), emap),
                  pl.BlockSpec((1, F, D), emap)],
        out_specs=pl.BlockSpec((nb, 1, C, D), bemap),
        out_shape=jax.ShapeDtypeStruct((B, E, C, D), bf16),
        compiler_params=_cparams(("parallel", "arbitrary")),
        name="moe_ffn",
    )(h, idx[..., None], gate[..., None].astype(f32), w1, w3, w2)


def _moe_scatter_kernel(ys_ref, idx_ref, x_ref, g_ref, o_ref, *, chunk):
    tt = x_ref.shape[1]
    EC = ys_ref.shape[1]
    t0 = pl.program_id(2) * tt
    tok = lax.broadcasted_iota(jnp.int32, (tt, chunk), 0) + t0
    acc = jnp.zeros(o_ref.shape[1:], f32)
    for s in range(EC // chunk):
        onehot = jnp.where(tok == idx_ref[0, :, s * chunk:(s + 1) * chunk], 1.0, 0.0).astype(bf16)
        acc = acc + jnp.dot(onehot, ys_ref[0, s * chunk:(s + 1) * chunk, :], preferred_element_type=f32)
    o_ref[0] = x_ref[0] + g_ref[0] * acc


def moe_scatter(ys, idx, x, g, *, tt=1024, dn=512):
    B, E, C, D = ys.shape
    T = x.shape[1]
    EC = E * C
    tt = _pick_tile(T, tt, 8)
    dn = _pick_tile(D, dn, LANES)
    chunk = _pick_tile(EC, 512, LANES)
    gmap = (lambda b, d, i: (b, 0, d)) if g.shape[0] > 1 else (lambda b, d, i: (0, 0, d))
    return pl.pallas_call(
        functools.partial(_moe_scatter_kernel, chunk=chunk), grid=(B, D // dn, T // tt),
        in_specs=[pl.BlockSpec((1, EC, dn), lambda b, d, i: (b, 0, d)),
                  pl.BlockSpec((1, 1, EC), lambda b, d, i: (b, 0, 0)),
                  pl.BlockSpec((1, tt, dn), lambda b, d, i: (b, i, d)),
                  pl.BlockSpec((1, 1, dn), gmap)],
        out_specs=pl.BlockSpec((1, tt, dn), lambda b, d, i: (b, i, d)),
        out_shape=jax.ShapeDtypeStruct((B, T, D), f32),
        compiler_params=_cparams(("parallel", "parallel", "arbitrary")),
        name="moe_scatter",
    )(ys.reshape(B, EC, D), idx.reshape(B, 1, EC), x, g.astype(f32))


def expert_choice_ffn(x, scale, shift, g, router, w1, w3, w2, *, expert_major):
    B, T, D = x.shape
    E = router.shape[1]
    cap = max(1, EC_FACTOR * T // E)
    router_p = jnp.pad(router, ((0, 0), (0, LANES - E))).astype(bf16)
    logits, h = linear(x, router_p, norm=True, scale=scale, shift=shift, emit_x=True)
    aff = jax.nn.softmax(logits[..., :E], axis=-1)
    gate, idx = lax.top_k(jnp.swapaxes(aff, 1, 2), cap)
    ys = moe_ffn(h, idx, gate, w1, w3, w2, expert_major=expert_major)
    return moe_scatter(ys, idx, x, g)


def _rms(x, g):
    return x * lax.rsqrt(jnp.mean(jnp.square(x), -1, keepdims=True) + EPS) * g


def _rope_tables(n_tok, rot_dim):
    t = jnp.arange(n_tok, dtype=jnp.int32)
    row = (t // GRID_W).astype(f32)
    col = (t % GRID_W).astype(f32)
    half = rot_dim // 2
    inv = ROPE_BASE ** (-jnp.arange(0, half, 2, dtype=f32) / half)
    ar = row[:, None] * inv
    ac = col[:, None] * inv
    ang = jnp.concatenate([ar, ar, ac, ac], -1)
    return jnp.cos(ang), jnp.sin(ang)


def even_mixer(x, ctx, mods, p, rope_cs, with_ctx_out):
    (scale, shift, gate), (cscale, cshift, cgate) = mods
    cut = (MLA_IN_PAD, 2 * CONV_CH)
    u_mla, u_cv = linear(x, p['w_in'], norm=True, scale=scale, shift=shift, splits=cut)
    uc_mla, uc_cv = linear(ctx, p['w_in'], norm=True, scale=cscale, shift=cshift, splits=cut)
    q, k, v = mla_prep(u_mla, p, rope_cs)
    qc, kc, vc = mla_prep(uc_mla, p, None)
    o_att = attention(q, [(k, v), (kc, vc)])
    post = (p['dw_b'], p['ln_g'], p['ln_b'])
    o_conv = dwconv(u_cv, p['dw_w'], glu=True, post=post)
    half = MLA_HEADS * MLA_V
    out = lambda res, g, oa, oc: linear(oa, p['w_out'][:half], x2=oc, w2=p['w_out'][half:], res=res, gate=g)
    x_new = out(x, gate, o_att, o_conv)
    if not with_ctx_out:
        return x_new, None
    oc_att = attention(qc, [(kc, vc)])
    oc_conv = dwconv(uc_cv, p['dw_w'], glu=True, post=post)
    return x_new, out(ctx, cgate, oc_att, oc_conv)


def _natten_ctx_qkv(u, p):
    B, T, _ = u.shape
    q, k, v = [t.reshape(B, T, NA_HEADS, NA_DIM) for t in jnp.split(u, 3, -1)]
    q = _rms(q, p['na_q_g']) * (NA_DIM ** -0.5)
    k = _rms(k, p['na_k_g'])
    z = jnp.zeros((B, T, NA_HEADS, LANES - NA_DIM), f32)
    ones_lane = z.at[..., 0].set(1.0)
    q, k, v = (jnp.concatenate([t, pad], -1) for t, pad in ((q, z), (k, z), (v, ones_lane)))
    return tuple(jnp.swapaxes(t, 1, 2).astype(bf16) for t in (q, k, v))


def odd_mixer(x, ctx, mods, p, with_ctx_out):
    (scale, shift, gate), (cscale, cshift, cgate) = mods
    cut = (RW_IN, NA_IN)
    u_rw, u_na = linear(x, p['w_in'], norm=True, scale=scale, shift=shift, splits=cut)
    uc_rw, uc_na = linear(ctx, p['w_in'], norm=True, scale=cscale, shift=cshift, splits=cut)
    y_rw, yc_rw = rwkv_mixer(u_rw, uc_rw, p, with_ctx_out)
    y_na = natten(u_na, uc_na, p['na_q_g'], p['na_k_g'], p['rpb'])
    out = lambda res, g, a, b: linear(a, p['w_out'][:RW_DIM], x2=b, w2=p['w_out'][RW_DIM:], res=res, gate=g)
    x_new = out(x, gate, y_rw, y_na)
    if not with_ctx_out:
        return x_new, None
    qc, kc, vc = _natten_ctx_qkv(uc_na, p)
    return x_new, out(ctx, cgate, yc_rw, attention(qc, [(kc, vc)]))


def _lora_by_direction(w):
    z = jnp.zeros_like(w[0])
    return jnp.stack([jnp.concatenate([w[0], z], 0), jnp.concatenate([z, w[1]], 0)]).astype(bf16)


def kernel(x, c, ctx, c_ctx, ada_w, ada_b, norm1_g, norm2_g, ev_w_in, ev_w_out, mla_q_norm, mla_w_uq, mla_kv_norm, mla_w_ukv, mla_q_g, mla_k_g, cv_dw_w, cv_dw_b, cv_ln_g, cv_ln_b, od_w_in, od_w_out, rw_shift_w, rw_w0, rw_w2, rw_a0, rw_a2, rw_g2, rw_k_k, rw_k_a, rw_r_k, rw_gn_g, rw_gn_b, na_q_g, na_k_g, na_rpb, moe_router, moe_w1, moe_w3, moe_w2):
    B, T, D = x.shape
    depth = ada_w.shape[0]
    rope_cs = _rope_tables(T, MLA_ROPE)
    cond = jax.nn.silu(jnp.concatenate([c, c_ctx[None]], 0))
    cond = jnp.pad(cond, ((0, (-(B + 1)) % 8), (0, 0)))[None]
    for i in range(depth):
        last = i == depth - 1
        j = i // 2
        mod = linear(cond, ada_w[i].astype(bf16), bias=ada_b[i][None])[0]
        sh1, sc1, g1, sh2, sc2, g2 = (t[:, None, :] for t in jnp.split(mod[:B], 6, -1))
        csh1, csc1, cg1, csh2, csc2, cg2 = (t[None] for t in jnp.split(mod[B:B + 1], 6, -1))
        n1, n2 = norm1_g[i], norm2_g[i]
        mods = ((n1 * (1.0 + sc1), sh1, g1), (n1 * (1.0 + csc1), csh1, cg1))
        if i % 2 == 0:
            w_in_mla, wq_blocks, wkv_blocks = _mla_weight_blocks(ev_w_in[j], mla_w_uq[j], mla_w_ukv[j])
            w_in = jnp.concatenate([w_in_mla, ev_w_in[j][:, MLA_IN:]], 1)
            p = dict(w_in=w_in.astype(bf16), w_out=ev_w_out[j].astype(bf16), q_norm=mla_q_norm[j],
                     wq_blocks=wq_blocks, kv_norm=mla_kv_norm[j], wkv_blocks=wkv_blocks,
                     q_g=mla_q_g[j], k_g=mla_k_g[j], dw_w=cv_dw_w[j], dw_b=cv_dw_b[j], ln_g=cv_ln_g[j], ln_b=cv_ln_b[j])
            x, ctx_mix = even_mixer(x, ctx, mods, p, rope_cs, not last)
        else:
            p = dict(w_in=od_w_in[j].astype(bf16), w_out=od_w_out[j].astype(bf16), shift_w=rw_shift_w[j],
                     w0=rw_w0[j], w2cat=_lora_by_direction(rw_w2[j]), a0=rw_a0[j], a2cat=_lora_by_direction(rw_a2[j]),
                     g2=rw_g2[j].astype(bf16), k_k=rw_k_k[j], k_a=rw_k_a[j], r_k=rw_r_k[j],
                     gn_g=rw_gn_g[j], gn_b=rw_gn_b[j], na_q_g=na_q_g[j], na_k_g=na_k_g[j], rpb=na_rpb[j])
            x, ctx_mix = odd_mixer(x, ctx, mods, p, not last)
        w1, w3, w2 = (w[i].astype(bf16) for w in (moe_w1, moe_w3, moe_w2))
        x = expert_choice_ffn(x, n2 * (1.0 + sc2), sh2, g2, moe_router[i], w1, w3, w2, expert_major=False)
        if not last:
            ctx = expert_choice_ffn(ctx_mix, n2 * (1.0 + csc2), csh2, cg2, moe_router[i], w1, w3, w2, expert_major=True)
    return x
```

```python
import functools
import math

import numpy as np
import jax
import jax.numpy as jnp
from jax import lax
from jax.experimental import pallas as pl
from jax.experimental.pallas import tpu as pltpu

f32 = jnp.float32
bf16 = jnp.bfloat16
HIGHEST = lax.Precision.HIGHEST

D_MODEL = 1024
GRID_W = 64
ROPE_BASE = 10000.0
EPS = 1e-6
LANES = 128

MLA_HEADS = D_MODEL // 128
MLA_NOPE = 64
MLA_ROPE = 32
MLA_QK = MLA_NOPE + MLA_ROPE
MLA_V = 64
Q_LORA = 3 * D_MODEL // 8
KV_LORA = D_MODEL // 4
MLA_IN = Q_LORA + KV_LORA + MLA_ROPE
MLA_IN_PAD = 768
CONV_CH = D_MODEL // 2
CONV_K = 31

RW_N = 64
RW_HEADS = D_MODEL // 128
RW_DIM = RW_HEADS * RW_N
DECAY_LORA = 64
ICLR_LORA = 64
GATE_LORA = 128
SHIFT_K = 3
RW_IN = 3 * RW_DIM + 2 * DECAY_LORA + 2 * ICLR_LORA + GATE_LORA
GN_EPS = 64e-5

NA_HEADS = D_MODEL // 128
NA_DIM = 64
WIN_H = 8
WIN_W = 16
NA_IN = 3 * NA_HEADS * NA_DIM
NA_QROWS = 4
NA_BAND = 12

N_EXPERTS = 16
EC_FACTOR = 2

WKV_CHUNK = 64
WKV_INV_BASE = 16
WKV_CHUNKS_PER_STEP = 4
WKV_SCAN_CHUNKS_PER_STEP = 8
CONV_HALO = 32
MASK_VALUE = -1e30

VMEM_LIMIT = 56 * 1024 * 1024


def _cparams(sem):
    return pltpu.CompilerParams(dimension_semantics=sem, vmem_limit_bytes=VMEM_LIMIT)


def _mm(a, b):
    return jnp.dot(a.astype(bf16), b.astype(bf16), preferred_element_type=f32)


def _mm_nt(a, b):
    return lax.dot_general(a.astype(bf16), b.astype(bf16), (((1,), (1,)), ((), ())), preferred_element_type=f32)


def _mm_tn(a, b):
    return lax.dot_general(a.astype(bf16), b.astype(bf16), (((0,), (0,)), ((), ())), preferred_element_type=f32)


def _head_masks():
    lane = lax.broadcasted_iota(jnp.int32, (1, LANES), 1)
    m0 = jnp.where(lane < LANES // 2, 1.0, 0.0).astype(f32)
    return m0, 1.0 - m0


def _head_sum(x, masks):
    s0 = jnp.sum(x * masks[0], axis=-1, keepdims=True)
    s1 = jnp.sum(x * masks[1], axis=-1, keepdims=True)
    return s0 * masks[0] + s1 * masks[1]


def _linear_kernel(*refs, norm, has_scale, has_shift, has_x2, has_bias, has_res, emit_x, splits):
    it = iter(refs)
    x_ref, w_ref = next(it), next(it)
    scale_ref = next(it) if has_scale else None
    shift_ref = next(it) if has_shift else None
    x2_ref = next(it) if has_x2 else None
    w2_ref = next(it) if has_x2 else None
    bias_ref = next(it) if has_bias else None
    res_ref = next(it) if has_res else None
    gate_ref = next(it) if has_res else None
    o_refs = [next(it) for _ in splits]
    xo_ref = next(it) if emit_x else None
    xb_ref = next(it)

    @pl.when(pl.program_id(2) == 0)
    def _():
        x = x_ref[0].astype(f32)
        if norm:
            x = x * lax.rsqrt(jnp.mean(x * x, axis=-1, keepdims=True) + EPS)
        if has_scale:
            x = x * scale_ref[0]
        if has_shift:
            x = x + shift_ref[0]
        xb_ref[...] = x.astype(bf16)
        if emit_x:
            xo_ref[0] = xb_ref[...]

    lo = 0
    for o_ref, width in zip(o_refs, splits):
        cols = slice(lo, lo + width) if len(splits) > 1 else slice(None)
        acc = jnp.dot(xb_ref[...], w_ref[:, cols], preferred_element_type=f32)
        if has_x2:
            acc = acc + jnp.dot(x2_ref[0].astype(bf16), w2_ref[:, cols], preferred_element_type=f32)
        if has_bias:
            acc = acc + bias_ref[:, cols]
        if has_res:
            acc = res_ref[0] + gate_ref[0] * acc
        o_ref[0] = acc.astype(o_ref.dtype)
        lo += width


def _pick_tile(n, target, align):
    if n <= target:
        return n
    t = (target // align) * align
    while t > align and n % t:
        t -= align
    assert n % t == 0, (n, target, align)
    return t


def linear(x, w, *, norm=False, scale=None, shift=None, x2=None, w2=None, bias=None,
           res=None, gate=None, emit_x=False, splits=None, tm=512, tn=None, out_dtype=f32):
    B, T, K = x.shape
    N = w.shape[1]
    assert w.shape[0] == K and N % LANES == 0, (x.shape, w.shape)
    tm = _pick_tile(T, tm, 8)
    if splits is None:
        tn = _pick_tile(N, 2048 if tn is None else tn, LANES)
        widths = (tn,)
    else:
        assert sum(splits) == N and all(s % LANES == 0 for s in splits) and res is None
        tn, widths = N, tuple(splits)
    grid = (B, T // tm, N // tn)

    def bvec(a):
        return (lambda b, i, j: (b, 0, 0)) if a.shape[0] > 1 else (lambda b, i, j: (0, 0, 0))

    args = [x, w]
    specs = [pl.BlockSpec((1, tm, K), lambda b, i, j: (b, i, 0)),
             pl.BlockSpec((K, tn), lambda b, i, j: (0, j))]
    if scale is not None:
        args.append(scale.astype(f32))
        specs.append(pl.BlockSpec((1, 1, K), bvec(scale)))
    if shift is not None:
        args.append(shift.astype(f32))
        specs.append(pl.BlockSpec((1, 1, K), bvec(shift)))
    if x2 is not None:
        K2 = x2.shape[-1]
        args += [x2, w2]
        specs += [pl.BlockSpec((1, tm, K2), lambda b, i, j: (b, i, 0)),
                  pl.BlockSpec((K2, tn), lambda b, i, j: (0, j))]
    if bias is not None:
        args.append(bias.astype(f32))
        specs.append(pl.BlockSpec((1, tn), lambda b, i, j: (0, j)))
    if res is not None:
        args += [res, gate.astype(f32)]
        specs += [pl.BlockSpec((1, tm, tn), lambda b, i, j: (b, i, j)),
                  pl.BlockSpec((1, 1, tn), (lambda b, i, j: (b, 0, j)) if gate.shape[0] > 1
                               else (lambda b, i, j: (0, 0, j)))]
    if splits is None:
        out_shape = [jax.ShapeDtypeStruct((B, T, N), out_dtype)]
        out_specs = [pl.BlockSpec((1, tm, tn), lambda b, i, j: (b, i, j))]
    else:
        out_shape = [jax.ShapeDtypeStruct((B, T, s), out_dtype) for s in splits]
        out_specs = [pl.BlockSpec((1, tm, s), lambda b, i, j: (b, i, 0)) for s in splits]
    if emit_x:
        out_shape.append(jax.ShapeDtypeStruct((B, T, K), bf16))
        out_specs.append(pl.BlockSpec((1, tm, K), lambda b, i, j: (b, i, 0)))
    kern = functools.partial(_linear_kernel, norm=norm, has_scale=scale is not None,
                             has_shift=shift is not None, has_x2=x2 is not None,
                             has_bias=bias is not None, has_res=res is not None, emit_x=emit_x, splits=widths)
    outs = pl.pallas_call(
        kern, grid=grid, in_specs=specs, out_specs=out_specs, out_shape=out_shape,
        scratch_shapes=[pltpu.VMEM((tm, K), bf16)],
        compiler_params=_cparams(("parallel", "parallel", "arbitrary")),
        name="linear",
    )(*args)
    return outs if (emit_x or splits is not None) else outs[0]


ATT_DV = LANES // 2


def _attn_kernel(*refs):
    q_ref, o_ref = refs[0], refs[-1]
    kv = refs[1:-1]
    lane = lax.broadcasted_iota(jnp.int32, (1, LANES), 1)
    res = []
    for h in range(2):
        q = q_ref[0, h]
        s = [_mm_nt(q, k_ref[0, h]) for k_ref in kv[0::2]]
        m = functools.reduce(jnp.maximum, [jnp.max(x, axis=-1, keepdims=True) for x in s])
        o = functools.reduce(jnp.add, [jnp.dot(jnp.exp((x - m).astype(bf16)), v_ref[0, h], preferred_element_type=f32)
                                       for x, v_ref in zip(s, kv[1::2])])
        res.append(o / o[:, ATT_DV:ATT_DV + 1])
    o_ref[0] = jnp.where(lane < ATT_DV, res[0], pltpu.roll(res[1], ATT_DV, axis=1))


def attention(q, kvs, *, tq=256):
    B, H, T, _ = q.shape
    assert H % 2 == 0
    tq = _pick_tile(T, tq, 8)
    args, specs = [q], [pl.BlockSpec((1, 2, tq, LANES), lambda b, h, i: (b, h, i, 0))]
    for k, v in kvs:
        args += [k, v]
        specs += [pl.BlockSpec((1, 2, k.shape[2], LANES), lambda b, h, i: (b, h, 0, 0))] * 2
    return pl.pallas_call(
        _attn_kernel, grid=(B, H // 2, T // tq), in_specs=specs,
        out_specs=pl.BlockSpec((1, tq, LANES), lambda b, h, i: (b, i, h)),
        out_shape=jax.ShapeDtypeStruct((B, T, H * ATT_DV), f32),
        compiler_params=_cparams(("parallel", "parallel", "arbitrary")),
        name="attention",
    )(*args)


def _mla_prep_kernel(u_ref, qn_ref, kvn_ref, wq_ref, wkv_ref, cq_ref, sq_ref, ck_ref, sk_ref,
                     q_ref, k_ref, v_ref):
    u = u_ref[0]
    H = q_ref.shape[1]

    def low_rank_norm(x, g_ref):
        return (x * lax.rsqrt(jnp.mean(x * x, axis=-1, keepdims=True) + EPS) * g_ref[...]).astype(bf16)

    qa = jnp.dot(low_rank_norm(u[:, :Q_LORA], qn_ref), wq_ref[...], preferred_element_type=f32)
    kva = jnp.dot(low_rank_norm(u[:, Q_LORA:Q_LORA + KV_LORA], kvn_ref), wkv_ref[...], preferred_element_type=f32)
    shared = pltpu.roll(u[:, Q_LORA + KV_LORA:], MLA_NOPE, axis=1)
    lane = lax.broadcasted_iota(jnp.int32, (1, LANES), 1)
    real = lane < MLA_QK
    ones_lane = jnp.where(lane == ATT_DV, 1.0, 0.0)

    def head_norm_rope(x, cos_ref, sin_ref, scale):
        ms = jnp.sum(jnp.where(real, x * x, 0.0), axis=-1, keepdims=True) * (1.0 / MLA_QK)
        y = x * cos_ref[...] + pltpu.roll(x, LANES - MLA_ROPE, axis=1) * sin_ref[...]
        return (y * (lax.rsqrt(ms + EPS) * scale)).astype(bf16)

    for h in range(H):
        cols = slice(h * LANES, (h + 1) * LANES)
        q_ref[0, h] = head_norm_rope(qa[:, cols], cq_ref, sq_ref, MLA_QK ** -0.5)
        k_ref[0, h] = head_norm_rope(kva[:, cols] + shared, ck_ref, sk_ref, 1.0)
        v_ref[0, h] = (kva[:, H * LANES + h * LANES:H * LANES + (h + 1) * LANES] + ones_lane).astype(bf16)


_ROPE_PERM = np.array([8, 9, 10, 11, 12, 13, 14, 15, 0, 1, 2, 3, 4, 5, 6, 7,
                       24, 25, 26, 27, 28, 29, 30, 31, 16, 17, 18, 19, 20, 21, 22, 23])
_ROPE_SIGN = np.array([-1.0] * 8 + [1.0] * 8 + [-1.0] * 8 + [1.0] * 8, np.float32)


def _mla_tables(g, rope_cs, n_tok):
    if rope_cs is None:
        cos, sin = jnp.ones((n_tok, MLA_ROPE), f32), jnp.zeros((n_tok, MLA_ROPE), f32)
    else:
        cos, sin = rope_cs
    pad = jnp.zeros((n_tok, LANES - MLA_QK), f32)
    c = jnp.concatenate([jnp.broadcast_to(g[:MLA_NOPE], (n_tok, MLA_NOPE)), g[MLA_NOPE:] * cos, pad], -1)
    s = jnp.concatenate([jnp.zeros((n_tok, MLA_NOPE), f32), g[MLA_NOPE:][_ROPE_PERM] * sin, pad], -1)
    return c, s


def mla_prep(u, p, rope_cs, *, tm=256):
    B, T, W = u.shape
    H = MLA_HEADS
    tm = _pick_tile(T, tm, 8)
    cq, sq = _mla_tables(p['q_g'], rope_cs, T)
    ck, sk = _mla_tables(p['k_g'], rope_cs, T)
    row = lambda a: pl.BlockSpec((1, a.shape[1]), lambda b, i: (0, 0))
    tab = pl.BlockSpec((tm, LANES), lambda b, i: (i, 0))
    full = lambda a: pl.BlockSpec(a.shape, lambda b, i: (0, 0))
    out = pl.BlockSpec((1, H, tm, LANES), lambda b, i: (b, 0, i, 0))
    shape = jax.ShapeDtypeStruct((B, H, T, LANES), bf16)
    qn, kvn = p['q_norm'][None], p['kv_norm'][None]
    return pl.pallas_call(
        _mla_prep_kernel, grid=(B, T // tm),
        in_specs=[pl.BlockSpec((1, tm, W), lambda b, i: (b, i, 0)), row(qn), row(kvn),
                  full(p['wq_blocks']), full(p['wkv_blocks']), tab, tab, tab, tab],
        out_specs=[out, out, out], out_shape=[shape, shape, shape],
        compiler_params=_cparams(("parallel", "parallel")),
        name="mla_prep",
    )(u, qn, kvn, p['wq_blocks'], p['wkv_blocks'], cq, sq, ck, sk)


def _mla_weight_blocks(w_in, w_uq, w_ukv):
    H = MLA_HEADS
    partner = lambda w: w[..., MLA_NOPE + _ROPE_PERM] * _ROPE_SIGN
    d = w_in.shape[0]
    kr = w_in[:, Q_LORA + KV_LORA:MLA_IN]
    w_in_mla = jnp.concatenate([w_in[:, :MLA_IN], kr[:, _ROPE_PERM] * _ROPE_SIGN,
                                jnp.zeros((d, MLA_IN_PAD - MLA_IN - MLA_ROPE), w_in.dtype)], 1)
    wq = w_uq.reshape(Q_LORA, H, MLA_QK)
    wq = jnp.concatenate([wq, partner(wq)], -1).reshape(Q_LORA, H * LANES)
    wkv = w_ukv.reshape(KV_LORA, H, MLA_NOPE + MLA_V)
    z = jnp.zeros((KV_LORA, H, LANES - MLA_NOPE), w_ukv.dtype)
    wk = jnp.concatenate([wkv[..., :MLA_NOPE], z], -1).reshape(KV_LORA, H * LANES)
    wv = jnp.concatenate([wkv[..., MLA_NOPE:], z], -1).reshape(KV_LORA, H * LANES)
    return w_in_mla, wq.astype(bf16), jnp.concatenate([wk, wv], 1).astype(bf16)


def _dwconv_kernel(*refs, taps, tb, C, glu, post):
    it = iter(refs)
    xp_ref, xm_ref, xn_ref, w_ref = next(it), next(it), next(it), next(it)
    if post:
        b_ref, g_ref, bb_ref = next(it), next(it), next(it)
    o_ref = next(it)
    hs_ref = next(it)
    i = pl.program_id(1)

    def pre(x):
        if glu:
            return x[:, :C] * jax.nn.sigmoid(x[:, C:])
        return x

    hs_ref[0:CONV_HALO, :] = pre(xp_ref[0]) * jnp.where(i > 0, 1.0, 0.0)
    hs_ref[CONV_HALO:CONV_HALO + tb, :] = pre(xm_ref[0])
    hs_ref[CONV_HALO + tb:, :] = pre(xn_ref[0]) * jnp.where(i < pl.num_programs(1) - 1, 1.0, 0.0)
    off = CONV_HALO - (taps - 1) // 2
    acc = jnp.zeros((tb, C), f32)
    for j in range(taps):
        acc = acc + w_ref[j:j + 1, :] * hs_ref[off + j:off + j + tb, :]
    if post:
        acc = acc + b_ref[...]
        mu = jnp.mean(acc, axis=-1, keepdims=True)
        d = acc - mu
        var = jnp.mean(d * d, axis=-1, keepdims=True)
        y = d * lax.rsqrt(var + EPS) * g_ref[...] + bb_ref[...]
        acc = y * jax.nn.sigmoid(y)
    o_ref[0] = acc


def dwconv(x, w, *, glu=False, post=None, tb=512):
    B, T, Cin = x.shape
    taps, C = w.shape
    assert Cin == (2 * C if glu else C) and (taps - 1) // 2 <= CONV_HALO
    tb = _pick_tile(T, tb, CONV_HALO)
    assert tb % CONV_HALO == 0
    r, nh = tb // CONV_HALO, T // CONV_HALO
    args = [x, x, x, w]
    specs = [pl.BlockSpec((1, CONV_HALO, Cin), lambda b, i: (b, jnp.maximum(i * r - 1, 0), 0)),
             pl.BlockSpec((1, tb, Cin), lambda b, i: (b, i, 0)),
             pl.BlockSpec((1, CONV_HALO, Cin), lambda b, i: (b, jnp.minimum((i + 1) * r, nh - 1), 0)),
             pl.BlockSpec((taps, C), lambda b, i: (0, 0))]
    if post is not None:
        args += [p.reshape(1, C) for p in post]
        specs += [pl.BlockSpec((1, C), lambda b, i: (0, 0))] * 3
    kern = functools.partial(_dwconv_kernel, taps=taps, tb=tb, C=C, glu=glu, post=post is not None)
    return pl.pallas_call(
        kern, grid=(B, T // tb), in_specs=specs,
        out_specs=pl.BlockSpec((1, tb, C), lambda b, i: (b, i, 0)),
        out_shape=jax.ShapeDtypeStruct((B, T, C), f32),
        scratch_shapes=[pltpu.VMEM((tb + 2 * CONV_HALO, C), f32)],
        compiler_params=_cparams(("parallel", "parallel")),
        name="dwconv",
    )(*args)


def _bmm(a, b):
    return jnp.einsum('gij,gjk->gik', a.astype(bf16), b.astype(bf16), preferred_element_type=f32)


def _bmm_nt(a, b):
    return jnp.einsum('gik,gjk->gij', a.astype(bf16), b.astype(bf16), preferred_element_type=f32)


def _bmm_tn(a, b):
    return lax.dot_general(a.astype(bf16), b.astype(bf16), (((1,), (1,)), ((0,), (0,))),
                           preferred_element_type=f32)


def _wkv_units(r, k_eff, v, lw, a, b, sgn, masks):
    G, Tc, _ = r.shape
    ti = lax.broadcasted_iota(jnp.int32, (G, Tc, Tc), 1)
    tj = lax.broadcasted_iota(jnp.int32, (G, Tc, Tc), 2)
    tri = jnp.where((tj - ti) * sgn <= 0, 1.0, 0.0).astype(bf16)
    hi = lw.astype(bf16)
    rest = lw - hi.astype(f32)
    mid = rest.astype(bf16)
    lo = (rest - mid.astype(f32)).astype(bf16)
    Lc = _bmm(tri, jnp.concatenate([hi, mid, lo], axis=2))
    L = Lc[:, :, :LANES] + Lc[:, :, LANES:2 * LANES] + Lc[:, :, 2 * LANES:]
    Ltot = jnp.sum(lw, axis=1, keepdims=True)
    enL = jnp.exp(-L)
    eR = jnp.exp(Ltot - L)
    At, Rt, Bt, Kt, Bb, Kb = a * jnp.exp(L - lw), r * jnp.exp(L), b * enL, k_eff * enL, b * eR, k_eff * eR

    stack = lambda x: jnp.concatenate([x * masks[0], x * masks[1]], axis=1)
    Y, X, Xk, Vs = stack(At), stack(Bt), stack(Kt), stack(v)
    n2 = 2 * Tc
    gi = lax.broadcasted_iota(jnp.int32, (G, n2, n2), 1)
    gj = lax.broadcasted_iota(jnp.int32, (G, n2, n2), 2)
    before = ((gj & (Tc - 1)) - (gi & (Tc - 1))) * sgn < 0
    scores = _bmm_nt(jnp.concatenate([Y, Rt], axis=1), jnp.concatenate([X, Xk], axis=1))
    Aab = jnp.where(before, scores[:, :n2, :n2], 0.0)
    Aak = jnp.where(before, scores[:, :n2, n2:], 0.0)
    ri = lax.broadcasted_iota(jnp.int32, (G, Tc, 2 * n2), 1)
    rj = lax.broadcasted_iota(jnp.int32, (G, Tc, 2 * n2), 2)
    upto = ((rj & (Tc - 1)) - ri) * sgn <= 0
    RB = jnp.where(upto, scores[:, n2:, :], 0.0)

    same = lambda size: jnp.right_shift(gi, int(math.log2(size))) == jnp.right_shift(gj, int(math.log2(size)))
    Xp = jnp.where(same(WKV_INV_BASE), Aab, 0.0)
    Tm = jnp.where(gi == gj, 1.0, 0.0) + Xp
    span = 2
    while span < WKV_INV_BASE:
        Xp = _bmm(Xp, Xp)
        Tm = Tm + _bmm(Tm, Xp)
        span *= 2
    size = WKV_INV_BASE
    while size < Tc:
        off = jnp.where(same(2 * size) & jnp.logical_not(same(size)), Aab, 0.0)
        Tm = Tm + _bmm(_bmm(Tm, off), Tm)
        size *= 2

    TA = _bmm(Tm, jnp.concatenate([_bmm(Aak, Vs), Y], axis=2))
    U0, Ah = TA[:, :, :LANES], TA[:, :, LANES:]
    UV = jnp.concatenate([U0, Vs], axis=1)
    Bs = stack(Bb)
    ki = lax.broadcasted_iota(jnp.int32, (G, LANES, LANES), 1)
    kj = lax.broadcasted_iota(jnp.int32, (G, LANES, LANES), 2)
    M = _bmm_tn(Bs, Ah) + jnp.where(ki == kj, jnp.exp(Ltot), 0.0)
    N = _bmm_tn(jnp.concatenate([Bs, stack(Kb)], axis=1), UV)
    return Rt + _bmm(RB[:, :, :n2], Ah), _bmm(RB, UV), M, N


def _wkv_chunk_kernel(r_ref, k_ref, v_ref, lw_ref, la_ref, w2_ref, w0_ref, a2_ref, a0_ref,
                      kk_ref, ka_ref, rk_ref, rh_ref, y0_ref, m_ref, n_ref):
    masks = _head_masks()
    P = r_ref.shape[2] // LANES
    CH = m_ref.shape[2]
    Tc = r_ref.shape[1] // CH
    def units(x):
        rows = (lambda j: slice(j * Tc, (j + 1) * Tc)) if x.shape[0] > 1 else (lambda j: slice(None))
        return jnp.stack([x[rows(j), q * LANES:(q + 1) * LANES] for q in range(P) for j in range(CH)], axis=0)

    lw_in, la_in = jnp.tanh(lw_ref[0]), la_ref[0]
    r, k, v = units(r_ref[0]), units(k_ref[0]), units(v_ref[0])
    kk = k * units(kk_ref[...])
    kk = kk * lax.rsqrt(_head_sum(kk * kk, masks) + 1e-12)
    lw, k_eff, b, bonus = [], [], [], []
    for d in range(2):
        logw = w0_ref[d] + _mm(lw_in, w2_ref[d])
        lw.append(units(-jnp.exp(-jax.nn.softplus(-logw) - 0.5)))
        rate = units(jax.nn.sigmoid(a0_ref[d] + _mm(la_in, a2_ref[d])))
        k_eff.append(k * (1.0 + (rate - 1.0) * units(ka_ref[...])))
        b.append(kk * rate)
        bonus.append(_head_sum(r * k_eff[d] * units(rk_ref[...]), masks) * v)
    both = lambda x: jnp.concatenate([x, x], axis=0)
    cat = lambda xs: jnp.concatenate(xs, axis=0)
    G = P * CH
    unit = lax.broadcasted_iota(jnp.int32, (2 * G, 1, 1), 0)
    sgn = jnp.where(unit < G, 1, -1)
    rh, y0, M, N = _wkv_units(both(r), cat(k_eff), both(v), cat(lw), both(-kk), cat(b), sgn, masks)
    y0 = y0 + cat(bonus)
    for d in range(2):
        for q in range(P):
            for j in range(CH):
                g, rows, cols = d * G + q * CH + j, slice(j * Tc, (j + 1) * Tc), slice(q * LANES, (q + 1) * LANES)
                rh_ref[d, 0, rows, cols] = rh[g].astype(rh_ref.dtype)
                y0_ref[d, 0, rows, cols] = y0[g]
                m_ref[d, 0, j, q] = M[g].astype(m_ref.dtype)
                n_ref[d, 0, j, q] = N[g]


def wkv_chunk(us, p):
    B, T, _ = us.shape
    Tc = WKV_CHUNK
    assert T % Tc == 0 and RW_IN % RW_DIM == 3 * LANES
    nc, P = T // Tc, RW_DIM // LANES
    CH = _pick_tile(nc, WKV_CHUNKS_PER_STEP, 1)
    wide = lambda i: pl.BlockSpec((1, CH * Tc, RW_DIM), lambda b, c: (b, c, i))
    lora_in = lambda i: pl.BlockSpec((1, CH * Tc, LANES), lambda b, c: (b, c, 3 * P + i))
    full = lambda a: pl.BlockSpec(a.shape, lambda b, c: (0,) * a.ndim)
    seq = pl.BlockSpec((2, 1, CH * Tc, RW_DIM), lambda b, c: (0, b, c, 0))
    mat = pl.BlockSpec((2, 1, CH, P, LANES, LANES), lambda b, c: (0, b, c, 0, 0, 0))
    params = [p['w2cat'], p['w0'][:, None, :], p['a2cat'], p['a0'][:, None, :],
              p['k_k'][None], p['k_a'][None], p['r_k'][None]]
    return pl.pallas_call(
        _wkv_chunk_kernel, grid=(B, nc // CH),
        in_specs=[wide(0), wide(1), wide(2), lora_in(0), lora_in(1)] + [full(a) for a in params],
        out_specs=[seq, seq, mat, mat],
        out_shape=[jax.ShapeDtypeStruct((2, B, T, RW_DIM), bf16), jax.ShapeDtypeStruct((2, B, T, RW_DIM), f32),
                   jax.ShapeDtypeStruct((2, B, nc, P, LANES, LANES), bf16),
                   jax.ShapeDtypeStruct((2, B, nc, P, LANES, LANES), f32)],
        compiler_params=_cparams(("parallel", "parallel")),
        name="wkv_chunk",
    )(us, us, us, us, us, *params)


def _wkv_scan_kernel(*refs, final, order):
    it = iter(refs)
    rh_ref, y0_ref, m_ref, n_ref, h0_ref = (next(it) for _ in range(5))
    if final:
        yo_ref, lg_ref, g2_ref, gg_ref, gb_ref = (next(it) for _ in range(5))
    y_ref, ht_ref, h_ref = next(it), next(it), next(it)
    c = pl.program_id(1)
    P = h_ref.shape[0]
    Tc = rh_ref.shape[2] // len(order)

    @pl.when(c == 0)
    def _():
        h_ref[...] = h0_ref[0]

    pairs = lambda x: jnp.stack([x[:, q * LANES:(q + 1) * LANES] for q in range(P)], axis=0)
    H = h_ref[...]
    if final:
        masks = _head_masks()
        gate = _mm(jax.nn.sigmoid(lg_ref[0]), g2_ref[...])
    for j in order:
        rows = slice(j * Tc, (j + 1) * Tc)
        y = pairs(y0_ref[0, 0, rows]) + _bmm(pairs(rh_ref[0, 0, rows]), H)
        H = _bmm(m_ref[0, 0, j], H) + n_ref[0, 0, j]
        if final:
            y = y + pairs(yo_ref[0, rows])
            mu = _head_sum(y, masks) * (1.0 / RW_N)
            dv = y - mu
            var = _head_sum(dv * dv, masks) * (1.0 / RW_N)
            y = (dv * lax.rsqrt(var + GN_EPS) * pairs(gg_ref[...]) + pairs(gb_ref[...])) * pairs(gate[rows])
        for q in range(P):
            y_ref[0, rows, q * LANES:(q + 1) * LANES] = y[q]
    h_ref[...] = H

    @pl.when(c == pl.num_programs(1) - 1)
    def _():
        ht_ref[0] = H


def wkv_scan(parts, h0, direction, final=None):
    rh, y0, m, n = parts
    _, B, T, _ = y0.shape
    Tc = WKV_CHUNK
    nc, P = T // Tc, RW_DIM // LANES
    S = _pick_tile(nc, WKV_SCAN_CHUNKS_PER_STEP, 1)
    ns = nc // S
    order = tuple(reversed(range(S))) if direction == 1 else tuple(range(S))
    ch = (lambda c: ns - 1 - c) if direction == 1 else (lambda c: c)
    seq = pl.BlockSpec((1, 1, S * Tc, RW_DIM), lambda b, c: (direction, b, ch(c), 0))
    mat = pl.BlockSpec((1, 1, S, P, LANES, LANES), lambda b, c: (direction, b, ch(c), 0, 0, 0))
    st = pl.BlockSpec((1, P, LANES, LANES), lambda b, c: (b, 0, 0, 0))
    out = pl.BlockSpec((1, S * Tc, RW_DIM), lambda b, c: (b, ch(c), 0))
    args, specs = [rh, y0, m, n, h0], [seq, seq, mat, mat, st]
    if final is not None:
        y_other, us, p = final
        vec = pl.BlockSpec((1, RW_DIM), lambda b, c: (0, 0))
        args += [y_other, us, p['g2'], p['gn_g'][None], p['gn_b'][None]]
        specs += [out, pl.BlockSpec((1, S * Tc, LANES), lambda b, c: (b, ch(c), RW_IN // LANES - 1)),
                  pl.BlockSpec((GATE_LORA, RW_DIM), lambda b, c: (0, 0)), vec, vec]
    return pl.pallas_call(
        functools.partial(_wkv_scan_kernel, final=final is not None, order=order), grid=(B, ns),
        in_specs=specs, out_specs=[out, st],
        out_shape=[jax.ShapeDtypeStruct((B, T, RW_DIM), f32), jax.ShapeDtypeStruct((B, P, LANES, LANES), f32)],
        scratch_shapes=[pltpu.VMEM((P, LANES, LANES), f32)],
        compiler_params=_cparams(("parallel", "arbitrary")),
        name="wkv_scan",
    )(*args)


def rwkv_mixer(u, uc, p, with_ctx_out):
    B = u.shape[0]
    us, usc = dwconv(u, p['shift_w']), dwconv(uc, p['shift_w'])
    lat, cx = wkv_chunk(us, p), wkv_chunk(usc, p)
    zero = jnp.zeros((B, RW_DIM // LANES, LANES, LANES), f32)
    yc_b, h_b = wkv_scan(cx, zero, 1)
    y_b, _ = wkv_scan(lat, h_b, 1)
    yc, h_f = wkv_scan(cx, zero, 0, final=(yc_b, usc, p) if with_ctx_out else None)
    y, _ = wkv_scan(lat, h_f, 0, final=(y_b, us, p))
    return y, (yc if with_ctx_out else None)


def _natten_kernel(q_ref, k0_ref, k1_ref, k2_ref, v0_ref, v1_ref, v2_ref, kc_ref, vc_ref,
                   b0_ref, b1_ref, qg_ref, kg_ref, o_ref):
    masks = _head_masks()
    nq = q_ref.shape[1]

    def norm(x, g):
        return x * lax.rsqrt(_head_sum(x * x, masks) * (1.0 / NA_DIM) + EPS) * g

    q = norm(q_ref[0], qg_ref[...]) * (NA_DIM ** -0.5)
    ks = [norm(r[0], kg_ref[...]).astype(bf16) for r in (k0_ref, k1_ref, k2_ref, kc_ref)]
    vs = [r[0].astype(bf16) for r in (v0_ref, v1_ref, v2_ref, vc_ref)]
    out = jnp.zeros(o_ref.shape[1:], f32)
    for h, bias_ref in enumerate((b0_ref, b1_ref)):
        qh = (q * masks[h]).astype(bf16)
        s = [_mm_nt(qh, ks[j]) + bias_ref[0, 0, :, j * nq:(j + 1) * nq] for j in range(3)]
        s.append(_mm_nt(qh, ks[3]))
        m = functools.reduce(jnp.maximum, [jnp.max(x, axis=-1, keepdims=True) for x in s])
        pr = [jnp.exp(x - m) for x in s]
        l = functools.reduce(jnp.add, [jnp.sum(x, axis=-1, keepdims=True) for x in pr])
        o = functools.reduce(jnp.add, [jnp.dot(pp.astype(bf16), vv, preferred_element_type=f32)
                                       for pp, vv in zip(pr, vs)])
        out = out + (o / l) * masks[h]
    o_ref[0] = out


def _natten_bias_table(rpb, rows):
    W = GRID_W
    kh, kw = min(WIN_H, rows), min(WIN_W, W)
    c = np.arange(W)[:, None]
    kc = np.arange(W)[None, :]
    cs = np.clip(c - kw // 2, 0, W - kw)
    col_ok = (kc >= cs) & (kc < cs + kw)
    col_hot = (col_ok[..., None] & ((kc - c + (WIN_W - 1))[..., None] == np.arange(2 * WIN_W - 1))).astype(np.float32)
    tabs = []
    for r0, bs in ((0, 0), (NA_QROWS, 0), (rows - NA_QROWS, rows - NA_BAND)):
        r = r0 + np.arange(NA_QROWS)[:, None]
        kr = bs + np.arange(NA_BAND)[None, :]
        rs = np.clip(r - kh // 2, 0, rows - kh)
        row_ok = (kr >= rs) & (kr < rs + kh)
        row_hot = (row_ok[..., None] & ((kr - r + (WIN_H - 1))[..., None] == np.arange(2 * WIN_H - 1))).astype(np.float32)
        t = jnp.einsum('rkd,cje,hde->hrckj', row_hot, col_hot, rpb, precision=HIGHEST)
        ok = row_ok[:, None, :, None] & col_ok[None, :, None, :]
        t = jnp.where(ok[None], t, MASK_VALUE)
        tabs.append(t.reshape(rpb.shape[0], NA_QROWS * W, NA_BAND * W))
    return jnp.stack(tabs, axis=1)


def natten(u, uc, q_g, k_g, rpb):
    B, T, _ = u.shape
    L = uc.shape[1]
    rows = T // GRID_W
    assert rows % NA_QROWS == 0 and rows >= NA_BAND and NA_BAND == 3 * NA_QROWS
    nb = rows // NA_QROWS
    nq = NA_QROWS * GRID_W
    P = NA_HEADS * NA_DIM // LANES
    table = _natten_bias_table(rpb.astype(f32), rows)

    def band(base, j):
        return pl.BlockSpec((1, nq, LANES), lambda b, q, i: (b, jnp.clip(i - 1, 0, nb - 3) + j, base + q))

    ctx = lambda base: pl.BlockSpec((1, L, LANES), lambda b, q, i: (b, 0, base + q))
    cls = lambda i: jnp.where(i == 0, 0, jnp.where(i == nb - 1, 2, 1))
    bias = lambda h: pl.BlockSpec((1, 1, nq, 3 * nq), lambda b, q, i: (2 * q + h, cls(i), 0, 0))
    vec = pl.BlockSpec((1, LANES), lambda b, q, i: (0, 0))
    two = lambda g: jnp.tile(g, 2)[None]
    return pl.pallas_call(
        _natten_kernel, grid=(B, P, nb),
        in_specs=[pl.BlockSpec((1, nq, LANES), lambda b, q, i: (b, i, q)),
                  band(P, 0), band(P, 1), band(P, 2), band(2 * P, 0), band(2 * P, 1), band(2 * P, 2),
                  ctx(P), ctx(2 * P), bias(0), bias(1), vec, vec],
        out_specs=pl.BlockSpec((1, nq, LANES), lambda b, q, i: (b, i, q)),
        out_shape=jax.ShapeDtypeStruct((B, T, NA_HEADS * NA_DIM), f32),
        compiler_params=_cparams(("parallel", "parallel", "arbitrary")),
        name="natten",
    )(u, u, u, u, u, u, u, uc, uc, table, table, two(q_g), two(k_g))


def _moe_ffn_kernel(h_ref, idx_ref, gate_ref, w1_ref, w3_ref, w2_ref, o_ref):
    nb, T = h_ref.shape[0], h_ref.shape[1]
    C = idx_ref.shape[2]
    tok = lax.broadcasted_iota(jnp.int32, (C, T), 1)
    xs = [jnp.dot(jnp.where(tok == idx_ref[s, 0], 1.0, 0.0).astype(bf16), h_ref[s],
                  preferred_element_type=f32).astype(bf16) for s in range(nb)]
    xs = jnp.concatenate(xs, axis=0) if nb > 1 else xs[0]
    a1 = jnp.dot(xs, w1_ref[0], preferred_element_type=f32)
    a3 = jnp.dot(xs, w3_ref[0], preferred_element_type=f32)
    hid = (a1 * jax.nn.sigmoid(a1) * a3).astype(bf16)
    ys = jnp.dot(hid, w2_ref[0], preferred_element_type=f32)
    for s in range(nb):
        o_ref[s, 0] = (ys[s * C:(s + 1) * C] * gate_ref[s, 0]).astype(o_ref.dtype)


def moe_ffn(h, idx, gate, w1, w3, w2, *, expert_major):
    B, T, D = h.shape
    E, C = idx.shape[1], idx.shape[2]
    F = w1.shape[2]
    if expert_major:
        nb, grid, be = B, (E, 1), (lambda e, b: (b, e))
    else:
        nb, grid, be = 1, (B, E), (lambda b, e: (b, e))
    bmap = lambda *g: (be(*g)[0], 0, 0)
    emap = lambda *g: (be(*g)[1], 0, 0)
    bemap = lambda *g: (*be(*g), 0, 0)
    return pl.pallas_call(
        _moe_ffn_kernel, grid=grid,
        in_specs=[pl.BlockSpec((nb, T, D), bmap),
                  pl.BlockSpec((nb, 1, C, 1), bemap),
                  pl.BlockSpec((nb, 1, C, 1), bemap),
                  pl.BlockSpec((1, D, F), emap),
                  pl.BlockSpec((1, D, F), emap),
                  pl.BlockSpec((1, F, D), emap)],
        out_specs=pl.BlockSpec((nb, 1, C, D), bemap),
        out_shape=jax.ShapeDtypeStruct((B, E, C, D), bf16),
        compiler_params=_cparams(("parallel", "arbitrary")),
        name="moe_ffn",
    )(h, idx[..., None], gate[..., None].astype(f32), w1, w3, w2)


def _moe_scatter_kernel(ys_ref, idx_ref, x_ref, g_ref, o_ref, *, chunk):
    tt = x_ref.shape[1]
    EC = ys_ref.shape[1]
    t0 = pl.program_id(2) * tt
    tok = lax.broadcasted_iota(jnp.int32, (tt, chunk), 0) + t0
    acc = jnp.zeros(o_ref.shape[1:], f32)
    for s in range(EC // chunk):
        onehot = jnp.where(tok == idx_ref[0, :, s * chunk:(s + 1) * chunk], 1.0, 0.0).astype(bf16)
        acc = acc + jnp.dot(onehot, ys_ref[0, s * chunk:(s + 1) * chunk, :], preferred_element_type=f32)
    o_ref[0] = x_ref[0] + g_ref[0] * acc


def moe_scatter(ys, idx, x, g, *, tt=1024, dn=512):
    B, E, C, D = ys.shape
    T = x.shape[1]
    EC = E * C
    tt = _pick_tile(T, tt, 8)
    dn = _pick_tile(D, dn, LANES)
    chunk = _pick_tile(EC, 512, LANES)
    gmap = (lambda b, d, i: (b, 0, d)) if g.shape[0] > 1 else (lambda b, d, i: (0, 0, d))
    return pl.pallas_call(
        functools.partial(_moe_scatter_kernel, chunk=chunk), grid=(B, D // dn, T // tt),
        in_specs=[pl.BlockSpec((1, EC, dn), lambda b, d, i: (b, 0, d)),
                  pl.BlockSpec((1, 1, EC), lambda b, d, i: (b, 0, 0)),
                  pl.BlockSpec((1, tt, dn), lambda b, d, i: (b, i, d)),
                  pl.BlockSpec((1, 1, dn), gmap)],
        out_specs=pl.BlockSpec((1, tt, dn), lambda b, d, i: (b, i, d)),
        out_shape=jax.ShapeDtypeStruct((B, T, D), f32),
        compiler_params=_cparams(("parallel", "parallel", "arbitrary")),
        name="moe_scatter",
    )(ys.reshape(B, EC, D), idx.reshape(B, 1, EC), x, g.astype(f32))


def expert_choice_ffn(x, scale, shift, g, router, w1, w3, w2, *, expert_major):
    B, T, D = x.shape
    E = router.shape[1]
    cap = max(1, EC_FACTOR * T // E)
    router_p = jnp.pad(router, ((0, 0), (0, LANES - E))).astype(bf16)
    logits, h = linear(x, router_p, norm=True, scale=scale, shift=shift, emit_x=True)
    aff = jax.nn.softmax(logits[..., :E], axis=-1)
    gate, idx = lax.top_k(jnp.swapaxes(aff, 1, 2), cap)
    ys = moe_ffn(h, idx, gate, w1, w3, w2, expert_major=expert_major)
    return moe_scatter(ys, idx, x, g)


def _rms(x, g):
    return x * lax.rsqrt(jnp.mean(jnp.square(x), -1, keepdims=True) + EPS) * g


def _rope_tables(n_tok, rot_dim):
    t = jnp.arange(n_tok, dtype=jnp.int32)
    row = (t // GRID_W).astype(f32)
    col = (t % GRID_W).astype(f32)
    half = rot_dim // 2
    inv = ROPE_BASE ** (-jnp.arange(0, half, 2, dtype=f32) / half)
    ar = row[:, None] * inv
    ac = col[:, None] * inv
    ang = jnp.concatenate([ar, ar, ac, ac], -1)
    return jnp.cos(ang), jnp.sin(ang)


def even_mixer(x, ctx, mods, p, rope_cs, with_ctx_out):
    (scale, shift, gate), (cscale, cshift, cgate) = mods
    cut = (MLA_IN_PAD, 2 * CONV_CH)
    u_mla, u_cv = linear(x, p['w_in'], norm=True, scale=scale, shift=shift, splits=cut)
    uc_mla, uc_cv = linear(ctx, p['w_in'], norm=True, scale=cscale, shift=cshift, splits=cut)
    q, k, v = mla_prep(u_mla, p, rope_cs)
    qc, kc, vc = mla_prep(uc_mla, p, None)
    o_att = attention(q, [(k, v), (kc, vc)])
    post = (p['dw_b'], p['ln_g'], p['ln_b'])
    o_conv = dwconv(u_cv, p['dw_w'], glu=True, post=post)
    half = MLA_HEADS * MLA_V
    out = lambda res, g, oa, oc: linear(oa, p['w_out'][:half], x2=oc, w2=p['w_out'][half:], res=res, gate=g)
    x_new = out(x, gate, o_att, o_conv)
    if not with_ctx_out:
        return x_new, None
    oc_att = attention(qc, [(kc, vc)])
    oc_conv = dwconv(uc_cv, p['dw_w'], glu=True, post=post)
    return x_new, out(ctx, cgate, oc_att, oc_conv)


def _natten_ctx_qkv(u, p):
    B, T, _ = u.shape
    q, k, v = [t.reshape(B, T, NA_HEADS, NA_DIM) for t in jnp.split(u, 3, -1)]
    q = _rms(q, p['na_q_g']) * (NA_DIM ** -0.5)
    k = _rms(k, p['na_k_g'])
    z = jnp.zeros((B, T, NA_HEADS, LANES - NA_DIM), f32)
    ones_lane = z.at[..., 0].set(1.0)
    q, k, v = (jnp.concatenate([t, pad], -1) for t, pad in ((q, z), (k, z), (v, ones_lane)))
    return tuple(jnp.swapaxes(t, 1, 2).astype(bf16) for t in (q, k, v))


def odd_mixer(x, ctx, mods, p, with_ctx_out):
    (scale, shift, gate), (cscale, cshift, cgate) = mods
    cut = (RW_IN, NA_IN)
    u_rw, u_na = linear(x, p['w_in'], norm=True, scale=scale, shift=shift, splits=cut)
    uc_rw, uc_na = linear(ctx, p['w_in'], norm=True, scale=cscale, shift=cshift, splits=cut)
    y_rw, yc_rw = rwkv_mixer(u_rw, uc_rw, p, with_ctx_out)
    y_na = natten(u_na, uc_na, p['na_q_g'], p['na_k_g'], p['rpb'])
    out = lambda res, g, a, b: linear(a, p['w_out'][:RW_DIM], x2=b, w2=p['w_out'][RW_DIM:], res=res, gate=g)
    x_new = out(x, gate, y_rw, y_na)
    if not with_ctx_out:
        return x_new, None
    qc, kc, vc = _natten_ctx_qkv(uc_na, p)
    return x_new, out(ctx, cgate, yc_rw, attention(qc, [(kc, vc)]))


def _lora_by_direction(w):
    z = jnp.zeros_like(w[0])
    return jnp.stack([jnp.concatenate([w[0], z], 0), jnp.concatenate([z, w[1]], 0)]).astype(bf16)


def kernel(x, c, ctx, c_ctx, ada_w, ada_b, norm1_g, norm2_g, ev_w_in, ev_w_out, mla_q_norm, mla_w_uq, mla_kv_norm, mla_w_ukv, mla_q_g, mla_k_g, cv_dw_w, cv_dw_b, cv_ln_g, cv_ln_b, od_w_in, od_w_out, rw_shift_w, rw_w0, rw_w2, rw_a0, rw_a2, rw_g2, rw_k_k, rw_k_a, rw_r_k, rw_gn_g, rw_gn_b, na_q_g, na_k_g, na_rpb, moe_router, moe_w1, moe_w3, moe_w2):
    B, T, D = x.shape
    depth = ada_w.shape[0]
    rope_cs = _rope_tables(T, MLA_ROPE)
    cond = jax.nn.silu(jnp.concatenate([c, c_ctx[None]], 0))
    cond = jnp.pad(cond, ((0, (-(B + 1)) % 8), (0, 0)))[None]
    for i in range(depth):
        last = i == depth - 1
        j = i // 2
        mod = linear(cond, ada_w[i].astype(bf16), bias=ada_b[i][None])[0]
        sh1, sc1, g1, sh2, sc2, g2 = (t[:, None, :] for t in jnp.split(mod[:B], 6, -1))
        csh1, csc1, cg1, csh2, csc2, cg2 = (t[None] for t in jnp.split(mod[B:B + 1], 6, -1))
        n1, n2 = norm1_g[i], norm2_g[i]
        mods = ((n1 * (1.0 + sc1), sh1, g1), (n1 * (1.0 + csc1), csh1, cg1))
        if i % 2 == 0:
            w_in_mla, wq_blocks, wkv_blocks = _mla_weight_blocks(ev_w_in[j], mla_w_uq[j], mla_w_ukv[j])
            w_in = jnp.concatenate([w_in_mla, ev_w_in[j][:, MLA_IN:]], 1)
            p = dict(w_in=w_in.astype(bf16), w_out=ev_w_out[j].astype(bf16), q_norm=mla_q_norm[j],
                     wq_blocks=wq_blocks, kv_norm=mla_kv_norm[j], wkv_blocks=wkv_blocks,
                     q_g=mla_q_g[j], k_g=mla_k_g[j], dw_w=cv_dw_w[j], dw_b=cv_dw_b[j], ln_g=cv_ln_g[j], ln_b=cv_ln_b[j])
            x, ctx_mix = even_mixer(x, ctx, mods, p, rope_cs, not last)
        else:
            p = dict(w_in=od_w_in[j].astype(bf16), w_out=od_w_out[j].astype(bf16), shift_w=rw_shift_w[j],
                     w0=rw_w0[j], w2cat=_lora_by_direction(rw_w2[j]), a0=rw_a0[j], a2cat=_lora_by_direction(rw_a2[j]),
                     g2=rw_g2[j].astype(bf16), k_k=rw_k_k[j], k_a=rw_k_a[j], r_k=rw_r_k[j],
                     gn_g=rw_gn_g[j], gn_b=rw_gn_b[j], na_q_g=na_q_g[j], na_k_g=na_k_g[j], rpb=na_rpb[j])
            x, ctx_mix = odd_mixer(x, ctx, mods, p, not last)
        w1, w3, w2 = (w[i].astype(bf16) for w in (moe_w1, moe_w3, moe_w2))
        x = expert_choice_ffn(x, n2 * (1.0 + sc2), sh2, g2, moe_router[i], w1, w3, w2, expert_major=False)
        if not last:
            ctx = expert_choice_ffn(ctx_mix, n2 * (1.0 + csc2), csh2, cg2, moe_router[i], w1, w3, w2, expert_major=True)
    return x
```
